```python
import jax
import jax.numpy as jnp
from jax import lax
import numpy as np

D_MODEL = 1024
BATCH = 32
SEQ = 256
DEPTH = 4
DEC_BATCH = 2
DEC_SEQ = 4096
PAST_LEN = 512

GRID_W = 64
N_EVEN = (DEPTH + 1) // 2
N_ODD = DEPTH // 2
N_ATTN_LAYERS = N_EVEN
N_MOD = 6
EPS = 1e-6

CONV_WIDTH = 512
CONV_KERNEL = 31
N_HEADS = 8
N_KV_HEADS = 2
HEAD_DIM = 64
ATTN_WIDTH = N_HEADS * HEAD_DIM
KV_WIDTH = N_KV_HEADS * HEAD_DIM
Q_BLOCK = 128
ROPE_THETA = 10000.0
FOURIER_GROUPS = 4
FOURIER_GROUP_DIM = 128
FOURIER_WIDTH = FOURIER_GROUPS * FOURIER_GROUP_DIM
SHORT_WIDTH = 512
SHORT_KERNEL = 3

EVEN_IN = 2 * CONV_WIDTH + ATTN_WIDTH + 2 * KV_WIDTH
EVEN_OUT = CONV_WIDTH + ATTN_WIDTH
ODD_IN = FOURIER_WIDTH + 3 * SHORT_WIDTH
ODD_OUT = FOURIER_WIDTH + SHORT_WIDTH

PEER_HEADS = 8
PEER_KEYS = 128
PEER_EXPERTS = PEER_KEYS * PEER_KEYS
PEER_QDIM = 256
PEER_HALF = PEER_QDIM // 2
PEER_TOPK = 16
PEER_TOKEN_BLOCK = 128

kernel_name = 'hybrid_diffusion_prefix_trunk_step'


def rms_norm(x, g):
    xf = x.astype(jnp.float32)
    y = xf * lax.rsqrt(jnp.mean(xf * xf, axis=-1, keepdims=True) + EPS)
    return (y * g.astype(jnp.float32)).astype(x.dtype)


def layer_norm(x, g, b):
    xf = x.astype(jnp.float32)
    mu = jnp.mean(xf, axis=-1, keepdims=True)
    xc = xf - mu
    var = jnp.mean(xc * xc, axis=-1, keepdims=True)
    y = xc * lax.rsqrt(var + EPS) * g.astype(jnp.float32) + b.astype(jnp.float32)
    return y.astype(x.dtype)


def depthwise_conv(x, w):
    return lax.conv_general_dilated(x, w[:, None, :], window_strides=(1,), padding='SAME',
                                    dimension_numbers=('NWC', 'WIO', 'NWC'),
                                    feature_group_count=x.shape[-1])


def axial_rope_tables(n_tokens):
    rows = n_tokens // GRID_W
    row = jnp.repeat(jnp.arange(rows), GRID_W).astype(jnp.float32)
    col = jnp.tile(jnp.arange(GRID_W), rows).astype(jnp.float32)
    half = HEAD_DIM // 2
    inv = ROPE_THETA ** (-jnp.arange(0, half, 2, dtype=jnp.float32) / half)
    ang = jnp.concatenate([row[:, None] * inv, col[:, None] * inv], axis=-1)
    return jnp.cos(ang), jnp.sin(ang)


def apply_rope(x, cos, sin):
    xf = x.astype(jnp.float32).reshape(*x.shape[:-1], HEAD_DIM // 2, 2)
    x1, x2 = xf[..., 0], xf[..., 1]
    cc = cos[None, :, None, :]
    ss = sin[None, :, None, :]
    out = jnp.stack([x1 * cc - x2 * ss, x1 * ss + x2 * cc], axis=-1)
    return out.reshape(x.shape).astype(x.dtype)


def block_attention(q, k, v):
    b, tq = q.shape[0], q.shape[1]
    grp = N_HEADS // N_KV_HEADS
    nblk = tq // Q_BLOCK
    qb = jnp.moveaxis(q.reshape(b, nblk, Q_BLOCK, N_KV_HEADS, grp, HEAD_DIM), 1, 0)
    kf = k.astype(jnp.float32)
    vf = v.astype(jnp.float32)
    scale = HEAD_DIM ** -0.5

    def one_block(qblk):
        s = jnp.einsum('bqkgd,bskd->bkgqs', qblk.astype(jnp.float32), kf) * scale
        p = jax.nn.softmax(s, axis=-1)
        return jnp.einsum('bkgqs,bskd->bqkgd', p, vf).astype(q.dtype)

    o = lax.map(one_block, qb)
    return jnp.moveaxis(o, 0, 1).reshape(b, tq, ATTN_WIDTH)


def even_mixer(h, p, ctx_k, ctx_v, rope):
    b, t, _ = h.shape
    proj = h @ p['w_in']
    s1 = CONV_WIDTH
    s2 = 2 * CONV_WIDTH
    s3 = s2 + ATTN_WIDTH
    s4 = s3 + KV_WIDTH
    a_val, a_gate, q, k, v = jnp.split(proj, [s1, s2, s3, s4], axis=-1)
    a = a_val * jax.nn.sigmoid(a_gate)
    a = depthwise_conv(a, p['conv_w']) + p['conv_b']
    a = jax.nn.silu(layer_norm(a, p['ln_g'], p['ln_b']))
    q = rms_norm(q.reshape(b, t, N_HEADS, HEAD_DIM), p['q_g'])
    k = rms_norm(k.reshape(b, t, N_KV_HEADS, HEAD_DIM), p['k_g'])
    v = v.reshape(b, t, N_KV_HEADS, HEAD_DIM)
    if rope is None:
        attn = block_attention(q, k, v)
    else:
        cos, sin = rope
        q = apply_rope(q, cos, sin)
        k = apply_rope(k, cos, sin)
        attn = block_attention(q, jnp.concatenate([ctx_k, k], axis=1),
                               jnp.concatenate([ctx_v, v], axis=1))
    out = jnp.concatenate([a, attn], axis=-1) @ p['w_out']
    return out, k, v


def odd_mixer(h, p):
    b, t, _ = h.shape
    proj = h @ p['w_in']
    f, g_b, g_c, hv = jnp.split(proj, [FOURIER_WIDTH, FOURIER_WIDTH + SHORT_WIDTH,
                                       FOURIER_WIDTH + 2 * SHORT_WIDTH], axis=-1)
    ff = f.reshape(b, t, FOURIER_GROUPS, FOURIER_GROUP_DIM).astype(jnp.float32)
    ff = jnp.fft.fft2(ff, axes=(1, 3), norm='ortho').real
    ff = ff.astype(h.dtype).reshape(b, t, FOURIER_WIDTH)
    sd = g_b * depthwise_conv(g_c * hv, p['short_w'])
    return jnp.concatenate([ff, sd], axis=-1) @ p['w_out']


def peer(h, wq, keys, u_tab, v_tab):
    b, t, d = h.shape
    blocks = h.reshape(-1, PEER_TOKEN_BLOCK, d)
    keys_f = keys.astype(jnp.float32)

    def one_block(xb):
        n = xb.shape[0]
        q = (xb @ wq).reshape(n, PEER_HEADS, 2, PEER_HALF).astype(jnp.float32)
        s = jnp.einsum('nhpd,hpkd->nhpk', q, keys_f)
        sv, si = lax.top_k(s, PEER_TOPK)
        cand = sv[:, :, 0, :, None] + sv[:, :, 1, None, :]
        cidx = si[:, :, 0, :, None] * PEER_KEYS + si[:, :, 1, None, :]
        cand = cand.reshape(n, PEER_HEADS, PEER_TOPK * PEER_TOPK)
        cidx = cidx.reshape(n, PEER_HEADS, PEER_TOPK * PEER_TOPK)
        fv, fi = lax.top_k(cand, PEER_TOPK)
        eidx = jnp.take_along_axis(cidx, fi, axis=-1)
        gate = jax.nn.softmax(fv, axis=-1)
        ue = jnp.take(u_tab, eidx, axis=0)
        act = jax.nn.gelu(jnp.einsum('nhkd,nd->nhk', ue, xb).astype(jnp.float32))
        ve = jnp.take(v_tab, eidx, axis=0)
        return jnp.einsum('nhk,nhkd->nd', (gate * act).astype(xb.dtype), ve)

    return lax.map(one_block, blocks).reshape(b, t, d)


def modulate(x, g, shift, scale):
    return rms_norm(x, g) * (1 + scale) + shift


def setup_inputs(seed: int = 0) -> dict:
    key = jax.random.key(seed)
    ks = jax.random.split(key, 32)
    f32 = jnp.float32

    def nrm(k, shape, scale):
        return jax.random.normal(k, shape, f32) * scale

    def gain(k, shape):
        return 1.0 + 0.01 * jax.random.normal(k, shape, f32)

    cache_shape = (DEC_BATCH, N_ATTN_LAYERS, PAST_LEN, N_KV_HEADS, HEAD_DIM)
    return {
        'x_prompt': nrm(ks[0], (BATCH, SEQ, D_MODEL), 1.0),
        'x_sample': nrm(ks[1], (DEC_BATCH, DEC_SEQ, D_MODEL), 1.0),
        'cache_k': nrm(ks[2], cache_shape, 1.0),
        'cache_v': nrm(ks[3], cache_shape, 1.0),
        'c': nrm(ks[4], (DEC_BATCH, D_MODEL), 1.0),
        'c_ctx': nrm(ks[5], (D_MODEL,), 1.0),
        'w_mod': nrm(ks[6], (DEPTH, D_MODEL, N_MOD * D_MODEL), 0.5 * D_MODEL ** -0.5),
        'b_mod': nrm(ks[7], (DEPTH, N_MOD * D_MODEL), 0.01),
        'norm_mix_g': gain(ks[8], (DEPTH, D_MODEL)),
        'norm_ffn_g': gain(ks[9], (DEPTH, D_MODEL)),
        'w_in_even': nrm(ks[10], (N_EVEN, D_MODEL, EVEN_IN), D_MODEL ** -0.5),
        'w_out_even': nrm(ks[11], (N_EVEN, EVEN_OUT, D_MODEL), EVEN_OUT ** -0.5),
        'conv_dw_w': nrm(ks[12], (N_EVEN, CONV_KERNEL, CONV_WIDTH), CONV_KERNEL ** -0.5),
        'conv_dw_b': nrm(ks[13], (N_EVEN, CONV_WIDTH), 0.01),
        'conv_ln_g': gain(ks[14], (N_EVEN, CONV_WIDTH)),
        'conv_ln_b': nrm(ks[15], (N_EVEN, CONV_WIDTH), 0.01),
        'q_norm_g': gain(ks[16], (N_EVEN, HEAD_DIM)),
        'k_norm_g': gain(ks[17], (N_EVEN, HEAD_DIM)),
        'w_in_odd': nrm(ks[18], (N_ODD, D_MODEL, ODD_IN), D_MODEL ** -0.5),
        'w_out_odd': nrm(ks[19], (N_ODD, ODD_OUT, D_MODEL), ODD_OUT ** -0.5),
        'short_conv_w': nrm(ks[20], (N_ODD, SHORT_KERNEL, SHORT_WIDTH), SHORT_KERNEL ** -0.5),
        'peer_wq': nrm(ks[21], (DEPTH, D_MODEL, PEER_HEADS * PEER_QDIM), D_MODEL ** -0.5),
        'peer_keys': nrm(ks[22], (DEPTH, PEER_HEADS, 2, PEER_KEYS, PEER_HALF), PEER_HALF ** -0.5),
        'peer_u': nrm(ks[23], (DEPTH, PEER_EXPERTS, D_MODEL), D_MODEL ** -0.5),
        'peer_v': nrm(ks[24], (DEPTH, PEER_EXPERTS, D_MODEL), 0.5),
        'final_norm_g': gain(ks[25], (D_MODEL,)),
    }


def reference(x_prompt, x_sample, cache_k, cache_v, c, c_ctx, w_mod, b_mod, norm_mix_g, norm_ffn_g,
              w_in_even, w_out_even, conv_dw_w, conv_dw_b, conv_ln_g, conv_ln_b, q_norm_g, k_norm_g,
              w_in_odd, w_out_odd, short_conv_w, peer_wq, peer_keys, peer_u, peer_v, final_norm_g):
    rope = axial_rope_tables(x_sample.shape[1])

    def modulation(l, cond):
        m = jax.nn.silu(cond) @ w_mod[l] + b_mod[l]
        return m.reshape(cond.shape[0], 1, N_MOD, D_MODEL)

    xc = x_prompt
    xs = x_sample
    new_k = []
    new_v = []
    for l in range(DEPTH):
        mc = modulation(l, c_ctx[None, :])
        ms = modulation(l, c)
        hc = modulate(xc, norm_mix_g[l], mc[:, :, 0], mc[:, :, 1])
        hs = modulate(xs, norm_mix_g[l], ms[:, :, 0], ms[:, :, 1])
        if l % 2 == 0:
            j = l // 2
            p = {'w_in': w_in_even[j], 'w_out': w_out_even[j], 'conv_w': conv_dw_w[j],
                 'conv_b': conv_dw_b[j], 'ln_g': conv_ln_g[j], 'ln_b': conv_ln_b[j],
                 'q_g': q_norm_g[j], 'k_g': k_norm_g[j]}
            oc, kc, vc = even_mixer(hc, p, None, None, None)
            new_k.append(kc)
            new_v.append(vc)
            o_s, _, _ = even_mixer(hs, p, cache_k[:, j], cache_v[:, j], rope)
        else:
            j = l // 2
            p = {'w_in': w_in_odd[j], 'w_out': w_out_odd[j], 'short_w': short_conv_w[j]}
            oc = odd_mixer(hc, p)
            o_s = odd_mixer(hs, p)
        xc = xc + mc[:, :, 2] * oc
        xs = xs + ms[:, :, 2] * o_s
        hc = modulate(xc, norm_ffn_g[l], mc[:, :, 3], mc[:, :, 4])
        hs = modulate(xs, norm_ffn_g[l], ms[:, :, 3], ms[:, :, 4])
        xc = xc + mc[:, :, 5] * peer(hc, peer_wq[l], peer_keys[l], peer_u[l], peer_v[l])
        xs = xs + ms[:, :, 5] * peer(hs, peer_wq[l], peer_keys[l], peer_u[l], peer_v[l])
    y_prompt = rms_norm(xc, final_norm_g)
    y_sample = rms_norm(xs, final_norm_g)
    new_cache_k = jnp.stack(new_k, axis=1)
    new_cache_v = jnp.stack(new_v, axis=1)
    return (y_prompt, y_sample, new_cache_k, new_cache_v)
```

```python
import functools
import math

import numpy as np
import jax
import jax.numpy as jnp
from jax import lax
from jax.experimental import pallas as pl
from jax.experimental.pallas import tpu as pltpu

F32 = jnp.float32
BF16 = jnp.bfloat16
EPS = 1e-6
NEG_INF = float("-inf")

GRID_W = 64
ROPE_THETA = 10000.0
N_MOD = 6
CONV_WIDTH = 512
CONV_KERNEL = 31
N_HEADS = 8
N_KV_HEADS = 2
HEAD_DIM = 64
ATTN_WIDTH = N_HEADS * HEAD_DIM
KV_WIDTH = N_KV_HEADS * HEAD_DIM
FOURIER_GROUPS = 4
FOURIER_GROUP_DIM = 128
FOURIER_WIDTH = FOURIER_GROUPS * FOURIER_GROUP_DIM
SHORT_WIDTH = 512
SHORT_KERNEL = 3
PEER_HEADS = 8
PEER_KEYS = 128
PEER_HALF = 128
PEER_TOPK = 16

LANES = 128
SUBLANES = 8
HALO = 16
TOKEN_BLOCK = 256
DENSE_TOKENS = 512
DENSE_EXPERTS = 1024
DENSE_SUB = 256
VMEM_LIMIT = 56 * 1024 * 1024

_NT = (((1,), (1,)), ((), ()))


def _cparams(*sem, vmem=VMEM_LIMIT):
    return pltpu.CompilerParams(dimension_semantics=sem, vmem_limit_bytes=vmem)


def _sigmoid(x):
    return 1.0 / (1.0 + jnp.exp(-x))


def _modulate(x, g, shift, scale):
    ms = jnp.mean(x * x, axis=-1, keepdims=True)
    y = x * lax.rsqrt(ms + EPS) * g
    return y * (1.0 + scale) + shift


def _group_of_block(i, n_ctx_blocks, blocks_per_latent):
    return jnp.where(i < n_ctx_blocks, 0, 1 + (i - n_ctx_blocks) // blocks_per_latent)


def _mod_kernel(cond_ref, w_ref, b_ref, o_ref):
    c = cond_ref[...]
    s = c * _sigmoid(c)
    o_ref[0] = jnp.dot(s, w_ref[0], preferred_element_type=F32) + b_ref[0]


def _modulation(cond, w_mod, b_mod):
    n_layers, d, f = w_mod.shape
    g = cond.shape[0]
    gp = -(-g // SUBLANES) * SUBLANES
    cond_p = jnp.zeros((gp, d), F32).at[:g].set(cond)
    cb = 1536
    out = pl.pallas_call(
        _mod_kernel,
        grid=(n_layers, f // cb),
        in_specs=[pl.BlockSpec((gp, d), lambda l, j: (0, 0)),
                  pl.BlockSpec((1, d, cb), lambda l, j: (l, 0, j)),
                  pl.BlockSpec((1, 1, cb), lambda l, j: (l, 0, j))],
        out_specs=pl.BlockSpec((1, gp, cb), lambda l, j: (l, 0, j)),
        out_shape=jax.ShapeDtypeStruct((n_layers, gp, f), F32),
        compiler_params=_cparams("parallel", "arbitrary"),
        name="modulation",
    )(cond_p, w_mod, b_mod.reshape(n_layers, 1, f))
    return out[:, :g].reshape(n_layers, g, N_MOD, d)


def _head_norm(t, bd, gain):
    tt = t * t
    hi = tt.astype(BF16)
    lo = (tt - hi.astype(F32)).astype(BF16)
    ss = (jnp.dot(hi, bd, preferred_element_type=F32) + jnp.dot(lo, bd, preferred_element_type=F32))
    return t * lax.rsqrt(ss * (1.0 / HEAD_DIM) + EPS) * gain


def _pair_swap(x):
    n = x.shape[-1]
    lane = lax.broadcasted_iota(jnp.int32, x.shape, 1)
    return jnp.where((lane & 1) == 0, pltpu.roll(x, n - 1, 1), pltpu.roll(x, 1, 1))


def _even_in_kernel(n_ctx_blocks, x_ref, mod_ref, g_ref, w_ref, qg_ref, kg_ref, bd_ref, cos_ref, sin_ref,
                    a_ref, q_ref, k_ref, v_ref):
    i = pl.program_id(0)
    h = _modulate(x_ref[...], g_ref[...], mod_ref[0, 0:1, :], mod_ref[0, 1:2, :])
    proj = jnp.dot(h.astype(BF16), w_ref[...], preferred_element_type=F32)
    s1, s2, s3, s4 = CONV_WIDTH, 2 * CONV_WIDTH, 2 * CONV_WIDTH + ATTN_WIDTH, 2 * CONV_WIDTH + ATTN_WIDTH + KV_WIDTH
    a_ref[...] = proj[:, :s1] * _sigmoid(proj[:, s1:s2])
    bd = bd_ref[...]
    qn = _head_norm(proj[:, s2:s3], bd, qg_ref[...])
    kn = _head_norm(proj[:, s3:s4], bd[:KV_WIDTH, :KV_WIDTH], kg_ref[...])
    v_ref[...] = proj[:, s4:]

    @pl.when(i < n_ctx_blocks)
    def _():
        q_ref[...] = qn.astype(BF16)
        k_ref[...] = kn

    @pl.when(i >= n_ctx_blocks)
    def _():
        cos = cos_ref[...]
        sin = sin_ref[...]
        q_ref[...] = (qn * cos + _pair_swap(qn) * sin).astype(BF16)
        k_ref[...] = kn * cos[:, :KV_WIDTH] + _pair_swap(kn) * sin[:, :KV_WIDTH]


def _even_in(x, mod, g, w_bf, qg, kg, bd, cos, sin, n_ctx, dec_seq):
    n, d = x.shape
    tm = TOKEN_BLOCK
    ncb, lb = n_ctx // tm, dec_seq // tm
    grp = lambda i: (_group_of_block(i, ncb, lb), 0, 0)
    rope_idx = lambda i: (jnp.maximum(i - ncb, 0) % lb, 0)
    full = lambda i: (0, 0)
    row = lambda i: (i, 0)
    f_in = w_bf.shape[1]
    return pl.pallas_call(
        functools.partial(_even_in_kernel, ncb),
        grid=(n // tm,),
        in_specs=[pl.BlockSpec((tm, d), row),
                  pl.BlockSpec((1, N_MOD, d), grp),
                  pl.BlockSpec((1, d), full),
                  pl.BlockSpec((d, f_in), full),
                  pl.BlockSpec((1, ATTN_WIDTH), full),
                  pl.BlockSpec((1, KV_WIDTH), full),
                  pl.BlockSpec((ATTN_WIDTH, ATTN_WIDTH), full),
                  pl.BlockSpec((tm, ATTN_WIDTH), rope_idx),
                  pl.BlockSpec((tm, ATTN_WIDTH), rope_idx)],
        out_specs=[pl.BlockSpec((tm, CONV_WIDTH), row),
                   pl.BlockSpec((tm, ATTN_WIDTH), row),
                   pl.BlockSpec((tm, KV_WIDTH), row),
                   pl.BlockSpec((tm, KV_WIDTH), row)],
        out_shape=[jax.ShapeDtypeStruct((n, CONV_WIDTH), F32),
                   jax.ShapeDtypeStruct((n, ATTN_WIDTH), BF16),
                   jax.ShapeDtypeStruct((n, KV_WIDTH), F32),
                   jax.ShapeDtypeStruct((n, KV_WIDTH), F32)],
        compiler_params=_cparams("parallel"),
        name="even_in",
    )(x, mod, g, w_bf, qg, kg, bd, cos, sin)


def _seq_position(i, n_ctx_blocks, ctx_seq_blocks, lat_seq_blocks):
    is_ctx = i < n_ctx_blocks
    sb = jnp.where(is_ctx, ctx_seq_blocks, lat_seq_blocks)
    pos = jnp.where(is_ctx, i, i - n_ctx_blocks) % sb
    return pos == 0, pos == sb - 1


def _fill_padded(pad_ref, prev, cur, nxt, first, last, tm):
    pad_ref[0:HALO, :] = jnp.where(first, 0.0, prev)
    pad_ref[HALO:HALO + tm, :] = cur
    pad_ref[HALO + tm:2 * HALO + tm, :] = jnp.where(last, 0.0, nxt)


def _conv_module_kernel(seq_info, prev_ref, cur_ref, next_ref, w_ref, b_ref, lg_ref, lb_ref, o_ref, pad_ref):
    tm = cur_ref.shape[0]
    first, last = _seq_position(pl.program_id(0), *seq_info)
    _fill_padded(pad_ref, prev_ref[...], cur_ref[...], next_ref[...], first, last, tm)
    half = CONV_KERNEL // 2
    acc = jnp.zeros(cur_ref.shape, F32)
    for kk in range(CONV_KERNEL):
        off = HALO - half + kk
        acc = acc + w_ref[kk:kk + 1, :] * pad_ref[off:off + tm, :]
    a = acc + b_ref[...]
    mu = jnp.mean(a, axis=-1, keepdims=True)
    xc = a - mu
    var = jnp.mean(xc * xc, axis=-1, keepdims=True)
    y = xc * lax.rsqrt(var + EPS) * lg_ref[...] + lb_ref[...]
    o_ref[...] = (y * _sigmoid(y)).astype(o_ref.dtype)


def _short_conv_kernel(seq_info, prev_ref, cur_ref, next_ref, gb_ref, w_ref, o_ref, pad_ref):
    tm = cur_ref.shape[0]
    first, last = _seq_position(pl.program_id(0), *seq_info)
    _fill_padded(pad_ref, prev_ref[...], cur_ref[...], next_ref[...], first, last, tm)
    half = SHORT_KERNEL // 2
    acc = jnp.zeros(cur_ref.shape, F32)
    for kk in range(SHORT_KERNEL):
        off = HALO - half + kk
        acc = acc + w_ref[kk:kk + 1, :] * pad_ref[off:off + tm, :]
    o_ref[...] = (gb_ref[...] * acc).astype(o_ref.dtype)


def _halo_specs(n, tm, width):
    r = tm // HALO
    nh = n // HALO
    return [pl.BlockSpec((HALO, width), lambda i: (jnp.maximum(i * r - 1, 0), 0)),
            pl.BlockSpec((tm, width), lambda i: (i, 0)),
            pl.BlockSpec((HALO, width), lambda i: (jnp.minimum((i + 1) * r, nh - 1), 0))]


def _conv_module(a, w, b, lg, lb, n_ctx, ctx_seq, dec_seq):
    n, width = a.shape
    tm = TOKEN_BLOCK
    seq_info = (n_ctx // tm, ctx_seq // tm, dec_seq // tm)
    full = lambda i: (0, 0)
    return pl.pallas_call(
        functools.partial(_conv_module_kernel, seq_info),
        grid=(n // tm,),
        in_specs=_halo_specs(n, tm, width) + [pl.BlockSpec((CONV_KERNEL, width), full)]
        + [pl.BlockSpec((1, width), full)] * 3,
        out_specs=pl.BlockSpec((tm, width), lambda i: (i, 0)),
        out_shape=jax.ShapeDtypeStruct((n, width), BF16),
        scratch_shapes=[pltpu.VMEM((tm + 2 * HALO, width), F32)],
        compiler_params=_cparams("parallel"),
        name="conv_module",
    )(a, a, a, w, b, lg, lb)


def _short_conv(cv, gb, w, n_ctx, ctx_seq, dec_seq):
    n, width = cv.shape
    tm = TOKEN_BLOCK
    seq_info = (n_ctx // tm, ctx_seq // tm, dec_seq // tm)
    return pl.pallas_call(
        functools.partial(_short_conv_kernel, seq_info),
        grid=(n // tm,),
        in_specs=_halo_specs(n, tm, width) + [pl.BlockSpec((tm, width), lambda i: (i, 0)),
                                              pl.BlockSpec((SHORT_KERNEL, width), lambda i: (0, 0))],
        out_specs=pl.BlockSpec((tm, width), lambda i: (i, 0)),
        out_shape=jax.ShapeDtypeStruct((n, width), BF16),
        scratch_shapes=[pltpu.VMEM((tm + 2 * HALO, width), F32)],
        compiler_params=_cparams("parallel"),
        name="short_conv",
    )(cv, cv, cv, gb, w)


def _attn_heads(q, key_sets, o_ref):
    scale = HEAD_DIM ** -0.5
    grp = N_HEADS // N_KV_HEADS
    for h in range(N_HEADS):
        g = h // grp
        qh = q[:, h * HEAD_DIM:(h + 1) * HEAD_DIM]
        scores = [lax.dot_general(qh, k[:, g * HEAD_DIM:(g + 1) * HEAD_DIM], _NT,
                                  preferred_element_type=F32) * scale for k, _ in key_sets]
        m = scores[0].max(axis=-1, keepdims=True)
        for s in scores[1:]:
            m = jnp.maximum(m, s.max(axis=-1, keepdims=True))
        den = jnp.zeros_like(m)
        num = jnp.zeros((q.shape[0], HEAD_DIM), F32)
        for s, (_, v) in zip(scores, key_sets):
            p = jnp.exp(s - m)
            den = den + p.sum(axis=-1, keepdims=True)
            num = num + jnp.dot(p.astype(BF16), v[:, g * HEAD_DIM:(g + 1) * HEAD_DIM],
                                preferred_element_type=F32)
        o_ref[:, h * HEAD_DIM:(h + 1) * HEAD_DIM] = (num / den).astype(o_ref.dtype)


def _attn_ctx_kernel(q_ref, k_ref, v_ref, o_ref):
    _attn_heads(q_ref[...], [(k_ref[...].astype(BF16), v_ref[...].astype(BF16))], o_ref)


def _attn_lat_kernel(q_ref, ck_ref, cv_ref, k_ref, v_ref, prev_ref, o_ref):
    del prev_ref
    _attn_heads(q_ref[...], [(ck_ref[0, 0].astype(BF16), cv_ref[0, 0].astype(BF16)),
                             (k_ref[...].astype(BF16), v_ref[...].astype(BF16))], o_ref)


def _attention(q, k, v, cache_k, cache_v, layer_j, n_ctx, ctx_seq, dec_seq):
    n = q.shape[0]
    out_shape = jax.ShapeDtypeStruct((n, ATTN_WIDTH), BF16)
    row = lambda b: (b, 0)
    ctx_out = pl.pallas_call(
        _attn_ctx_kernel,
        grid=(n_ctx // ctx_seq,),
        in_specs=[pl.BlockSpec((ctx_seq, ATTN_WIDTH), row),
                  pl.BlockSpec((ctx_seq, KV_WIDTH), row),
                  pl.BlockSpec((ctx_seq, KV_WIDTH), row)],
        out_specs=pl.BlockSpec((ctx_seq, ATTN_WIDTH), row),
        out_shape=out_shape,
        compiler_params=_cparams("parallel"),
        name="attn_ctx",
    )(q, k, v)
    tq = TOKEN_BLOCK
    dec_batch = (n - n_ctx) // dec_seq
    past = cache_k.shape[2]
    qidx = lambda b, t: (n_ctx // tq + b * (dec_seq // tq) + t, 0)
    kidx = lambda b, t: (n_ctx // dec_seq + b, 0)
    cidx = lambda b, t: (b, layer_j, 0, 0)
    return pl.pallas_call(
        _attn_lat_kernel,
        grid=(dec_batch, dec_seq // tq),
        in_specs=[pl.BlockSpec((tq, ATTN_WIDTH), qidx),
                  pl.BlockSpec((1, 1, past, KV_WIDTH), cidx),
                  pl.BlockSpec((1, 1, past, KV_WIDTH), cidx),
                  pl.BlockSpec((dec_seq, KV_WIDTH), kidx),
                  pl.BlockSpec((dec_seq, KV_WIDTH), kidx),
                  pl.BlockSpec(memory_space=pl.ANY)],
        out_specs=pl.BlockSpec((tq, ATTN_WIDTH), qidx),
        out_shape=out_shape,
        input_output_aliases={5: 0},
        compiler_params=_cparams("parallel", "arbitrary"),
        name="attn_lat",
    )(q, cache_k, cache_v, k, v, ctx_out)


def _out_proj_kernel(x_ref, mod_ref, y1_ref, y2_ref, w1_ref, w2_ref, o_ref):
    y = (jnp.dot(y1_ref[...], w1_ref[...], preferred_element_type=F32)
         + jnp.dot(y2_ref[...], w2_ref[...], preferred_element_type=F32))
    o_ref[...] = x_ref[...] + mod_ref[0, 2:3, :] * y


def _out_proj(x, mod, y1, y2, w_bf, n_ctx, dec_seq):
    n, d = x.shape
    tm = TOKEN_BLOCK
    ncb, lb = n_ctx // tm, dec_seq // tm
    w1, w2 = y1.shape[1], y2.shape[1]
    row = lambda i: (i, 0)
    return pl.pallas_call(
        _out_proj_kernel,
        grid=(n // tm,),
        in_specs=[pl.BlockSpec((tm, d), row),
                  pl.BlockSpec((1, N_MOD, d), lambda i: (_group_of_block(i, ncb, lb), 0, 0)),
                  pl.BlockSpec((tm, w1), row),
                  pl.BlockSpec((tm, w2), row),
                  pl.BlockSpec((w1, d), lambda i: (0, 0)),
                  pl.BlockSpec((w2, d), lambda i: (1, 0))],
        out_specs=pl.BlockSpec((tm, d), row),
        out_shape=jax.ShapeDtypeStruct((n, d), F32),
        compiler_params=_cparams("parallel"),
        name="out_proj",
    )(x, mod, y1, y2, w_bf, w_bf)


def _odd_in_kernel(x_ref, mod_ref, g_ref, w_ref, f_ref, gb_ref, cv_ref):
    h = _modulate(x_ref[...], g_ref[...], mod_ref[0, 0:1, :], mod_ref[0, 1:2, :])
    proj = jnp.dot(h.astype(BF16), w_ref[...], preferred_element_type=F32)
    fw, sw = FOURIER_WIDTH, SHORT_WIDTH
    f_ref[...] = proj[:, :fw].astype(BF16)
    gb_ref[...] = proj[:, fw:fw + sw]
    cv_ref[...] = proj[:, fw + sw:fw + 2 * sw] * proj[:, fw + 2 * sw:]


def _odd_in(x, mod, g, w_bf, n_ctx, dec_seq):
    n, d = x.shape
    tm = TOKEN_BLOCK
    ncb, lb = n_ctx // tm, dec_seq // tm
    row = lambda i: (i, 0)
    full = lambda i: (0, 0)
    return pl.pallas_call(
        _odd_in_kernel,
        grid=(n // tm,),
        in_specs=[pl.BlockSpec((tm, d), row),
                  pl.BlockSpec((1, N_MOD, d), lambda i: (_group_of_block(i, ncb, lb), 0, 0)),
                  pl.BlockSpec((1, d), full),
                  pl.BlockSpec((d, w_bf.shape[1]), full)],
        out_specs=[pl.BlockSpec((tm, FOURIER_WIDTH), row),
                   pl.BlockSpec((tm, SHORT_WIDTH), row),
                   pl.BlockSpec((tm, SHORT_WIDTH), row)],
        out_shape=[jax.ShapeDtypeStruct((n, FOURIER_WIDTH), BF16),
                   jax.ShapeDtypeStruct((n, SHORT_WIDTH), F32),
                   jax.ShapeDtypeStruct((n, SHORT_WIDTH), F32)],
        compiler_params=_cparams("parallel"),
        name="odd_in",
    )(x, mod, g, w_bf)


def _dft_table_kernel(ac_ref, as_ref, bc_ref, bs_ref, c_ref, s_ref):
    ac = ac_ref[0]
    a_s = as_ref[0]
    bc = bc_ref[...]
    bs = bs_ref[...]
    c_ref[...] = (ac * bc - a_s * bs).astype(BF16)
    s_ref[...] = (a_s * bc + ac * bs).astype(BF16)


def _dft_tables(t):
    fine = 64
    coarse = t // fine
    k = np.arange(t, dtype=np.int64)
    ang_a = 2.0 * np.pi * ((np.arange(coarse, dtype=np.int64)[:, None] * fine * k[None, :]) % t) / t
    ang_b = 2.0 * np.pi * ((np.arange(fine, dtype=np.int64)[:, None] * k[None, :]) % t) / t
    ac = jnp.asarray(np.cos(ang_a), F32).reshape(coarse, 1, t)
    a_s = jnp.asarray(np.sin(ang_a), F32).reshape(coarse, 1, t)
    bc = jnp.asarray(np.cos(ang_b), F32)
    bs = jnp.asarray(np.sin(ang_b), F32)
    a_spec = pl.BlockSpec((1, 1, t), lambda i: (i, 0, 0))
    b_spec = pl.BlockSpec((fine, t), lambda i: (0, 0))
    o_spec = pl.BlockSpec((fine, t), lambda i: (i, 0))
    return pl.pallas_call(
        _dft_table_kernel,
        grid=(coarse,),
        in_specs=[a_spec, a_spec, b_spec, b_spec],
        out_specs=[o_spec, o_spec],
        out_shape=[jax.ShapeDtypeStruct((t, t), BF16)] * 2,
        compiler_params=_cparams("parallel"),
        name="dft_tables",
    )(ac, a_s, bc, bs)


def _small_dft(n):
    jk = (np.arange(n, dtype=np.int64)[:, None] * np.arange(n, dtype=np.int64)[None, :]) % n
    ang = 2.0 * np.pi * jk / n
    return np.cos(ang), np.sin(ang)


def _channel_dft(f, cb, sb):
    xc = jnp.dot(f, cb, preferred_element_type=F32).astype(BF16)
    xs = jnp.dot(f, sb, preferred_element_type=F32).astype(BF16)
    return xc, xs


def _fourier_ctx_kernel(scale, f_ref, cb_ref, sb_ref, ct_ref, st_ref, o_ref):
    xc, xs = _channel_dft(f_ref[...], cb_ref[...], sb_ref[...])
    y = (jnp.dot(ct_ref[...], xc, preferred_element_type=F32)
         - jnp.dot(st_ref[...], xs, preferred_element_type=F32))
    o_ref[...] = (y * scale).astype(o_ref.dtype)


def _fourier_lat_kernel(scale, f_ref, cb_ref, sb_ref, ct_ref, st_ref, prev_ref, o_ref, xc_ref, xs_ref):
    del prev_ref

    @pl.when(pl.program_id(1) == 0)
    def _():
        xc, xs = _channel_dft(f_ref[...], cb_ref[...], sb_ref[...])
        xc_ref[...] = xc
        xs_ref[...] = xs

    y = (jnp.dot(ct_ref[...], xc_ref[...], preferred_element_type=F32)
         - jnp.dot(st_ref[...], xs_ref[...], preferred_element_type=F32))
    o_ref[...] = (y * scale).astype(o_ref.dtype)


def _fourier(f, cb, sb, ct_ctx, st_ctx, ct_lat, st_lat, n_ctx, ctx_seq, dec_seq):
    n, width = f.shape
    out_shape = jax.ShapeDtypeStruct((n, width), BF16)
    full = lambda *_: (0, 0)
    ctx_out = pl.pallas_call(
        functools.partial(_fourier_ctx_kernel, 1.0 / math.sqrt(ctx_seq * FOURIER_GROUP_DIM)),
        grid=(n_ctx // ctx_seq,),
        in_specs=[pl.BlockSpec((ctx_seq, width), lambda b: (b, 0)),
                  pl.BlockSpec((width, width), full),
                  pl.BlockSpec((width, width), full),
                  pl.BlockSpec((ctx_seq, ctx_seq), full),
                  pl.BlockSpec((ctx_seq, ctx_seq), full)],
        out_specs=pl.BlockSpec((ctx_seq, width), lambda b: (b, 0)),
        out_shape=out_shape,
        compiler_params=_cparams("parallel"),
        name="fourier_ctx",
    )(f, cb, sb, ct_ctx, st_ctx)
    tr = TOKEN_BLOCK
    dec_batch = (n - n_ctx) // dec_seq
    fidx = lambda b, r: (n_ctx // dec_seq + b, 0)
    oidx = lambda b, r: (n_ctx // tr + b * (dec_seq // tr) + r, 0)
    return pl.pallas_call(
        functools.partial(_fourier_lat_kernel, 1.0 / math.sqrt(dec_seq * FOURIER_GROUP_DIM)),
        grid=(dec_batch, dec_seq // tr),
        in_specs=[pl.BlockSpec((dec_seq, width), fidx),
                  pl.BlockSpec((width, width), full),
                  pl.BlockSpec((width, width), full),
                  pl.BlockSpec((tr, dec_seq), lambda b, r: (r, 0)),
                  pl.BlockSpec((tr, dec_seq), lambda b, r: (r, 0)),
                  pl.BlockSpec(memory_space=pl.ANY)],
        out_specs=pl.BlockSpec((tr, width), oidx),
        out_shape=out_shape,
        scratch_shapes=[pltpu.VMEM((dec_seq, width), BF16)] * 2,
        input_output_aliases={5: 0},
        compiler_params=_cparams("parallel", "arbitrary"),
        name="fourier_lat",
    )(f, cb, sb, ct_lat, st_lat, ctx_out)


def _count_eq(cur, m):
    return jnp.sum(jnp.where(cur == m, 1.0, 0.0), axis=0, keepdims=True)


def _top_values(s, n_out, n_rows):
    t = s.shape[1]
    row = lax.broadcasted_iota(jnp.int32, (n_rows, t), 0).astype(F32)
    out = jnp.full((n_rows, t), NEG_INF, F32)
    cnt = jnp.zeros((1, t), F32)
    cur = s
    for _ in range(n_out):
        m = jnp.max(cur, axis=0, keepdims=True)
        new_cnt = cnt + _count_eq(cur, m)
        out = jnp.where(row >= cnt, jnp.where(row < new_cnt, m, out), out)
        cur = jnp.where(cur == m, NEG_INF, cur)
        cnt = new_cnt
    return jnp.where(row < float(n_out), out, NEG_INF)


def _kth_pair(cand, k):
    t = cand.shape[1]
    cnt = jnp.zeros((1, t), F32)
    lo = jnp.full((1, t), NEG_INF, F32)
    hi = jnp.full((1, t), NEG_INF, F32)
    cur = cand
    for _ in range(k + 1):
        m = jnp.max(cur, axis=0, keepdims=True)
        new_cnt = cnt + _count_eq(cur, m)
        hi = jnp.where(cnt < float(k), jnp.where(new_cnt >= float(k), m, hi), hi)
        lo = jnp.where(cnt < float(k + 1), jnp.where(new_cnt >= float(k + 1), m, lo), lo)
        cur = jnp.where(cur == m, NEG_INF, cur)
        cnt = new_cnt
    return hi, lo


def _candidate_sums(a, b):
    k1 = PEER_TOPK + 1
    t = a.shape[1]
    row8 = lax.broadcasted_iota(jnp.int32, (SUBLANES, t), 0)
    b0, b1, b2 = b[0:8], b[8:16], b[16:24]
    parts = [a[0:1] + b0, a[0:1] + b1, a[0:1] + b2]
    for p in range(1, SUBLANES):
        keep = k1 // (p + 1)
        parts.append(jnp.where(row8 < keep, a[p:p + 1] + b0, NEG_INF))
    parts.append(a[8:16] + b[0:1])
    parts.append(a[16:24] + b[0:1])
    return jnp.concatenate(parts, axis=0)


def _peer_select_kernel(x_ref, mod_ref, g_ref, wq_ref, keys_ref, h_ref, thr_ref, e1_ref, s2_ref, e2_ref, q_scr):
    h2 = _modulate(x_ref[...], g_ref[...], mod_ref[0, 3:4, :], mod_ref[0, 4:5, :]).astype(BF16)
    h_ref[...] = h2
    q = jnp.dot(h2, wq_ref[...], preferred_element_type=F32).astype(BF16)
    for c in range(2 * PEER_HEADS):
        q_scr[c] = q[:, c * PEER_HALF:(c + 1) * PEER_HALF]

    def head(h, carry):
        s1 = lax.dot_general(keys_ref[2 * h], q_scr[2 * h], _NT, preferred_element_type=F32)
        s2 = lax.dot_general(keys_ref[2 * h + 1], q_scr[2 * h + 1], _NT, preferred_element_type=F32)
        k1 = PEER_TOPK + 1
        a = _top_values(s1, k1, 24)
        b = _top_values(s2, k1, 24)
        cand = _candidate_sums(a, b)
        c16, c17 = _kth_pair(cand, PEER_TOPK)
        tmid = 0.5 * (c16 + c17)
        cmax = a[0:1] + b[0:1]
        z = jnp.sum(jnp.where(cand >= tmid, jnp.exp(cand - cmax), 0.0), axis=0, keepdims=True)
        thr_ref[h] = tmid - s1
        e1_ref[h] = jnp.exp(s1 - a[0:1]) / z
        s2_ref[h] = s2
        e2_ref[h] = jnp.exp(s2 - b[0:1])
        return carry

    lax.fori_loop(0, PEER_HEADS, head, 0)


def _peer_select(x, mod, g, wq_bf, keys_bf, n_ctx, dec_seq):
    n, d = x.shape
    tm = TOKEN_BLOCK
    ncb, lb = n_ctx // tm, dec_seq // tm
    row = lambda i: (i, 0)
    sel = lambda i: (0, 0, i)
    sel_shape = jax.ShapeDtypeStruct((PEER_HEADS, PEER_KEYS, n), F32)
    sel_spec = pl.BlockSpec((PEER_HEADS, PEER_KEYS, tm), sel)
    return pl.pallas_call(
        _peer_select_kernel,
        grid=(n // tm,),
        in_specs=[pl.BlockSpec((tm, d), row),
                  pl.BlockSpec((1, N_MOD, d), lambda i: (_group_of_block(i, ncb, lb), 0, 0)),
                  pl.BlockSpec((1, d), lambda i: (0, 0)),
                  pl.BlockSpec(wq_bf.shape, lambda i: (0, 0)),
                  pl.BlockSpec(keys_bf.shape, lambda i: (0, 0, 0))],
        out_specs=[pl.BlockSpec((tm, d), row), sel_spec, sel_spec, sel_spec, sel_spec],
        out_shape=[jax.ShapeDtypeStruct((n, d), BF16), sel_shape, sel_shape, sel_shape, sel_shape],
        scratch_shapes=[pltpu.VMEM((2 * PEER_HEADS, tm, PEER_HALF), BF16)],
        compiler_params=_cparams("parallel"),
        name="peer_select",
    )(x, mod, g, wq_bf, keys_bf)


def _gelu_tanh(x):
    c = math.sqrt(2.0 / math.pi)
    return x * (0.5 * (1.0 + jnp.tanh(c * (x + 0.044715 * (x * x * x)))))


def _peer_dense_kernel(x_ref, mod_ref, h_ref, u_ref, vt_ref, thr_ref, e1_ref, s2_ref, e2_ref, o_ref, acc_ref):
    eb = pl.program_id(1)

    @pl.when(eb == 0)
    def _():
        acc_ref[...] = jnp.zeros_like(acc_ref)

    h2 = h_ref[...]
    keys_per_step = DENSE_EXPERTS // PEER_KEYS
    keys_per_sub = DENSE_SUB // PEER_KEYS
    tot = None
    for sb in range(DENSE_EXPERTS // DENSE_SUB):
        u = u_ref[sb * DENSE_SUB:(sb + 1) * DENSE_SUB, :]
        act = _gelu_tanh(lax.dot_general(u, h2, _NT, preferred_element_type=F32))
        ws = []
        for il in range(keys_per_sub):
            ig = eb * keys_per_step + sb * keys_per_sub + il
            w = None
            for h in range(PEER_HEADS):
                hit = s2_ref[h] >= thr_ref[h, pl.ds(ig, 1), :]
                term = jnp.where(hit, e2_ref[h], 0.0) * e1_ref[h, pl.ds(ig, 1), :]
                w = term if w is None else w + term
            ws.append(w)
        gt = (jnp.concatenate(ws, axis=0) * act).astype(BF16)
        part = jnp.dot(vt_ref[:, sb * DENSE_SUB:(sb + 1) * DENSE_SUB], gt, preferred_element_type=F32)
        tot = part if tot is None else tot + part
    acc_ref[...] += tot

    @pl.when(eb == pl.num_programs(1) - 1)
    def _():
        o_ref[...] = x_ref[...] + mod_ref[0, 5:6, :] * acc_ref[...].T


def _peer_dense(x, mod, h2, u_bf, vt_bf, thr, e1, s2, e2, n_ctx, dec_seq):
    n, d = x.shape
    tb, eb = DENSE_TOKENS, DENSE_EXPERTS
    n_exp = u_bf.shape[0]
    ncb, lb = n_ctx // tb, dec_seq // tb
    row = lambda t, e: (t, 0)
    sel_spec = pl.BlockSpec((PEER_HEADS, PEER_KEYS, tb), lambda t, e: (0, 0, t))
    return pl.pallas_call(
        _peer_dense_kernel,
        grid=(n // tb, n_exp // eb),
        in_specs=[pl.BlockSpec((tb, d), row),
                  pl.BlockSpec((1, N_MOD, d), lambda t, e: (_group_of_block(t, ncb, lb), 0, 0)),
                  pl.BlockSpec((tb, d), row),
                  pl.BlockSpec((eb, d), lambda t, e: (e, 0)),
                  pl.BlockSpec((d, eb), lambda t, e: (0, e)),
                  sel_spec, sel_spec, sel_spec, sel_spec],
        out_specs=pl.BlockSpec((tb, d), row),
        out_shape=jax.ShapeDtypeStruct((n, d), F32),
        scratch_shapes=[pltpu.VMEM((d, tb), F32)],
        compiler_params=_cparams("parallel", "arbitrary"),
        name="peer_dense",
    )(x, mod, h2, u_bf, vt_bf, thr, e1, s2, e2)


def _final_norm_kernel(x_ref, g_ref, o_ref):
    x = x_ref[...]
    ms = jnp.mean(x * x, axis=-1, keepdims=True)
    o_ref[...] = x * lax.rsqrt(ms + EPS) * g_ref[...]


def _final_norm(x, g):
    n, d = x.shape
    tm = TOKEN_BLOCK
    return pl.pallas_call(
        _final_norm_kernel,
        grid=(n // tm,),
        in_specs=[pl.BlockSpec((tm, d), lambda i: (i, 0)), pl.BlockSpec((1, d), lambda i: (0, 0))],
        out_specs=pl.BlockSpec((tm, d), lambda i: (i, 0)),
        out_shape=jax.ShapeDtypeStruct((n, d), F32),
        compiler_params=_cparams("parallel"),
        name="final_norm",
    )(x, g)


def _rope_tables(n_tokens):
    rows = n_tokens // GRID_W
    row = np.repeat(np.arange(rows), GRID_W).astype(np.float32)
    col = np.tile(np.arange(GRID_W), rows).astype(np.float32)
    half = HEAD_DIM // 2
    inv = (np.float32(ROPE_THETA) ** (-np.arange(0, half, 2, dtype=np.float32) / np.float32(half))).astype(np.float32)
    ang = np.concatenate([row[:, None] * inv, col[:, None] * inv], axis=-1).astype(np.float32)
    cos = np.repeat(np.cos(ang.astype(np.float64)), 2, axis=-1)
    sin = np.repeat(np.sin(ang.astype(np.float64)), 2, axis=-1)
    sin = sin * np.tile(np.array([-1.0, 1.0]), HEAD_DIM // 2)
    return (jnp.asarray(np.tile(cos, (1, N_HEADS)), F32), jnp.asarray(np.tile(sin, (1, N_HEADS)), F32))


def _block_diag(block, count):
    n = block.shape[0]
    out = np.zeros((n * count, n * count), np.float64)
    for c in range(count):
        out[c * n:(c + 1) * n, c * n:(c + 1) * n] = block
    return out


def kernel(x_prompt, x_sample, cache_k, cache_v, c, c_ctx, w_mod, b_mod, norm_mix_g, norm_ffn_g, w_in_even, w_out_even, conv_dw_w, conv_dw_b, conv_ln_g, conv_ln_b, q_norm_g, k_norm_g, w_in_odd, w_out_odd, short_conv_w, peer_wq, peer_keys, peer_u, peer_v, final_norm_g):
    batch, ctx_seq, d = x_prompt.shape
    dec_batch, dec_seq, _ = x_sample.shape
    depth = w_mod.shape[0]
    n_ctx = batch * ctx_seq
    past = cache_k.shape[2]

    x = jnp.concatenate([x_prompt.reshape(n_ctx, d), x_sample.reshape(dec_batch * dec_seq, d)], axis=0)
    mods = _modulation(jnp.concatenate([c_ctx[None, :], c], axis=0), w_mod, b_mod)

    cos, sin = _rope_tables(dec_seq)
    bd = jnp.asarray(_block_diag(np.ones((HEAD_DIM, HEAD_DIM)), N_HEADS), BF16)
    c128, s128 = _small_dft(FOURIER_GROUP_DIM)
    cb = jnp.asarray(_block_diag(c128, FOURIER_GROUPS), BF16)
    sb = jnp.asarray(_block_diag(s128, FOURIER_GROUPS), BF16)
    c_ctx_t, s_ctx_t = _small_dft(ctx_seq)
    ct_ctx, st_ctx = jnp.asarray(c_ctx_t, BF16), jnp.asarray(s_ctx_t, BF16)
    if depth > 1:
        ct_lat, st_lat = _dft_tables(dec_seq)
    ck = cache_k.reshape(dec_batch, -1, past, KV_WIDTH)
    cv = cache_v.reshape(dec_batch, -1, past, KV_WIDTH)

    new_k, new_v = [], []
    for l in range(depth):
        mod = mods[l]
        j = l // 2
        g_mix = norm_mix_g[l][None, :]
        if l % 2 == 0:
            qg = jnp.tile(q_norm_g[j], N_HEADS)[None, :]
            kg = jnp.tile(k_norm_g[j], N_KV_HEADS)[None, :]
            a, q, k, v = _even_in(x, mod, g_mix, w_in_even[j].astype(BF16), qg, kg, bd, cos, sin, n_ctx, dec_seq)
            new_k.append(k[:n_ctx].reshape(batch, ctx_seq, N_KV_HEADS, HEAD_DIM))
            new_v.append(v[:n_ctx].reshape(batch, ctx_seq, N_KV_HEADS, HEAD_DIM))
            y1 = _conv_module(a, conv_dw_w[j], conv_dw_b[j][None, :], conv_ln_g[j][None, :],
                              conv_ln_b[j][None, :], n_ctx, ctx_seq, dec_seq)
            y2 = _attention(q, k, v, ck, cv, j, n_ctx, ctx_seq, dec_seq)
            x = _out_proj(x, mod, y1, y2, w_out_even[j].astype(BF16), n_ctx, dec_seq)
        else:
            f, gb, cvv = _odd_in(x, mod, g_mix, w_in_odd[j].astype(BF16), n_ctx, dec_seq)
            y1 = _fourier(f, cb, sb, ct_ctx, st_ctx, ct_lat, st_lat, n_ctx, ctx_seq, dec_seq)
            y2 = _short_conv(cvv, gb, short_conv_w[j], n_ctx, ctx_seq, dec_seq)
            x = _out_proj(x, mod, y1, y2, w_out_odd[j].astype(BF16), n_ctx, dec_seq)
        keys_bf = peer_keys[l].reshape(2 * PEER_HEADS, PEER_KEYS, PEER_HALF).astype(BF16)
        h2, thr, e1, s2, e2 = _peer_select(x, mod, norm_ffn_g[l][None, :], peer_wq[l].astype(BF16), keys_bf,
                                           n_ctx, dec_seq)
        x = _peer_dense(x, mod, h2, peer_u[l].astype(BF16), peer_v[l].T.astype(BF16), thr, e1, s2, e2,
                        n_ctx, dec_seq)

    y = _final_norm(x, final_norm_g[None, :])
    y_prompt = y[:n_ctx].reshape(batch, ctx_seq, d)
    y_sample = y[n_ctx:].reshape(dec_batch, dec_seq, d)
    return (y_prompt, y_sample, jnp.stack(new_k, axis=1), jnp.stack(new_v, axis=1))
```

```python
import functools
import math

import numpy as np
import jax
import jax.numpy as jnp
from jax import lax
from jax.experimental import pallas as pl
from jax.experimental.pallas import tpu as pltpu

F32 = jnp.float32
BF16 = jnp.bfloat16
EPS = 1e-6
NEG_INF = float("-inf")

GRID_W = 64
ROPE_THETA = 10000.0
N_MOD = 6
CONV_WIDTH = 512
CONV_KERNEL = 31
N_HEADS = 8
N_KV_HEADS = 2
HEAD_DIM = 64
ATTN_WIDTH = N_HEADS * HEAD_DIM
KV_WIDTH = N_KV_HEADS * HEAD_DIM
FOURIER_GROUPS = 4
FOURIER_GROUP_DIM = 128
FOURIER_WIDTH = FOURIER_GROUPS * FOURIER_GROUP_DIM
SHORT_WIDTH = 512
SHORT_KERNEL = 3
PEER_HEADS = 8
PEER_KEYS = 128
PEER_HALF = 128
PEER_TOPK = 16

LANES = 128
SUBLANES = 8
HALO = 16
TOKEN_BLOCK = 256
DENSE_TOKENS = 512
DENSE_EXPERTS = 1024
DENSE_SUB = 256
VMEM_LIMIT = 56 * 1024 * 1024

_NT = (((1,), (1,)), ((), ()))


def _cparams(*sem, vmem=VMEM_LIMIT):
    return pltpu.CompilerParams(dimension_semantics=sem, vmem_limit_bytes=vmem)


def _sigmoid(x):
    return 1.0 / (1.0 + jnp.exp(-x))


def _modulate(x, g, shift, scale):
    ms = jnp.mean(x * x, axis=-1, keepdims=True)
    y = x * lax.rsqrt(ms + EPS) * g
    return y * (1.0 + scale) + shift


def _group_of_block(i, n_ctx_blocks, blocks_per_latent):
    return jnp.where(i < n_ctx_blocks, 0, 1 + (i - n_ctx_blocks) // blocks_per_latent)


def _mod_kernel(cond_ref, w_ref, b_ref, o_ref):
    c = cond_ref[...]
    s = c * _sigmoid(c)
    o_ref[0] = jnp.dot(s, w_ref[0], preferred_element_type=F32) + b_ref[0]


def _modulation(cond, w_mod, b_mod):
    n_layers, d, f = w_mod.shape
    g = cond.shape[0]
    gp = -(-g // SUBLANES) * SUBLANES
    cond_p = jnp.zeros((gp, d), F32).at[:g].set(cond)
    cb = 1536
    out = pl.pallas_call(
        _mod_kernel,
        grid=(n_layers, f // cb),
        in_specs=[pl.BlockSpec((gp, d), lambda l, j: (0, 0)),
                  pl.BlockSpec((1, d, cb), lambda l, j: (l, 0, j)),
                  pl.BlockSpec((1, 1, cb), lambda l, j: (l, 0, j))],
        out_specs=pl.BlockSpec((1, gp, cb), lambda l, j: (l, 0, j)),
        out_shape=jax.ShapeDtypeStruct((n_layers, gp, f), F32),
        compiler_params=_cparams("parallel", "arbitrary"),
        name="modulation",
    )(cond_p, w_mod, b_mod.reshape(n_layers, 1, f))
    return out[:, :g].reshape(n_layers, g, N_MOD, d)


def _head_norm(t, bd, gain):
    tt = t * t
    hi = tt.astype(BF16)
    lo = (tt - hi.astype(F32)).astype(BF16)
    ss = (jnp.dot(hi, bd, preferred_element_type=F32) + jnp.dot(lo, bd, preferred_element_type=F32))
    return t * lax.rsqrt(ss * (1.0 / HEAD_DIM) + EPS) * gain


def _pair_swap(x):
    n = x.shape[-1]
    lane = lax.broadcasted_iota(jnp.int32, x.shape, 1)
    return jnp.where((lane & 1) == 0, pltpu.roll(x, n - 1, 1), pltpu.roll(x, 1, 1))


def _even_in_kernel(n_ctx_blocks, x_ref, mod_ref, g_ref, w_ref, qg_ref, kg_ref, bd_ref, cos_ref, sin_ref,
                    a_ref, q_ref, k_ref, v_ref):
    i = pl.program_id(0)
    h = _modulate(x_ref[...], g_ref[...], mod_ref[0, 0:1, :], mod_ref[0, 1:2, :])
    proj = jnp.dot(h.astype(BF16), w_ref[...], preferred_element_type=F32)
    s1, s2, s3, s4 = CONV_WIDTH, 2 * CONV_WIDTH, 2 * CONV_WIDTH + ATTN_WIDTH, 2 * CONV_WIDTH + ATTN_WIDTH + KV_WIDTH
    a_ref[...] = proj[:, :s1] * _sigmoid(proj[:, s1:s2])
    bd = bd_ref[...]
    qn = _head_norm(proj[:, s2:s3], bd, qg_ref[...])
    kn = _head_norm(proj[:, s3:s4], bd[:KV_WIDTH, :KV_WIDTH], kg_ref[...])
    v_ref[...] = proj[:, s4:]

    @pl.when(i < n_ctx_blocks)
    def _():
        q_ref[...] = qn.astype(BF16)
        k_ref[...] = kn

    @pl.when(i >= n_ctx_blocks)
    def _():
        cos = cos_ref[...]
        sin = sin_ref[...]
        q_ref[...] = (qn * cos + _pair_swap(qn) * sin).astype(BF16)
        k_ref[...] = kn * cos[:, :KV_WIDTH] + _pair_swap(kn) * sin[:, :KV_WIDTH]


def _even_in(x, mod, g, w_bf, qg, kg, bd, cos, sin, n_ctx, dec_seq):
    n, d = x.shape
    tm = TOKEN_BLOCK
    ncb, lb = n_ctx // tm, dec_seq // tm
    grp = lambda i: (_group_of_block(i, ncb, lb), 0, 0)
    rope_idx = lambda i: (jnp.maximum(i - ncb, 0) % lb, 0)
    full = lambda i: (0, 0)
    row = lambda i: (i, 0)
    f_in = w_bf.shape[1]
    return pl.pallas_call(
        functools.partial(_even_in_kernel, ncb),
        grid=(n // tm,),
        in_specs=[pl.BlockSpec((tm, d), row),
                  pl.BlockSpec((1, N_MOD, d), grp),
                  pl.BlockSpec((1, d), full),
                  pl.BlockSpec((d, f_in), full),
                  pl.BlockSpec((1, ATTN_WIDTH), full),
                  pl.BlockSpec((1, KV_WIDTH), full),
                  pl.BlockSpec((ATTN_WIDTH, ATTN_WIDTH), full),
                  pl.BlockSpec((tm, ATTN_WIDTH), rope_idx),
                  pl.BlockSpec((tm, ATTN_WIDTH), rope_idx)],
        out_specs=[pl.BlockSpec((tm, CONV_WIDTH), row),
                   pl.BlockSpec((tm, ATTN_WIDTH), row),
                   pl.BlockSpec((tm, KV_WIDTH), row),
                   pl.BlockSpec((tm, KV_WIDTH), row)],
        out_shape=[jax.ShapeDtypeStruct((n, CONV_WIDTH), F32),
                   jax.ShapeDtypeStruct((n, ATTN_WIDTH), BF16),
                   jax.ShapeDtypeStruct((n, KV_WIDTH), F32),
                   jax.ShapeDtypeStruct((n, KV_WIDTH), F32)],
        compiler_params=_cparams("parallel"),
        name="even_in",
    )(x, mod, g, w_bf, qg, kg, bd, cos, sin)


def _seq_position(i, n_ctx_blocks, ctx_seq_blocks, lat_seq_blocks):
    is_ctx = i < n_ctx_blocks
    sb = jnp.where(is_ctx, ctx_seq_blocks, lat_seq_blocks)
    pos = jnp.where(is_ctx, i, i - n_ctx_blocks) % sb
    return pos == 0, pos == sb - 1


def _fill_padded(pad_ref, prev, cur, nxt, first, last, tm):
    pad_ref[0:HALO, :] = jnp.where(first, 0.0, prev)
    pad_ref[HALO:HALO + tm, :] = cur
    pad_ref[HALO + tm:2 * HALO + tm, :] = jnp.where(last, 0.0, nxt)


def _conv_module_kernel(seq_info, prev_ref, cur_ref, next_ref, w_ref, b_ref, lg_ref, lb_ref, o_ref, pad_ref):
    tm = cur_ref.shape[0]
    first, last = _seq_position(pl.program_id(0), *seq_info)
    _fill_padded(pad_ref, prev_ref[...], cur_ref[...], next_ref[...], first, last, tm)
    half = CONV_KERNEL // 2
    acc = jnp.zeros(cur_ref.shape, F32)
    for kk in range(CONV_KERNEL):
        off = HALO - half + kk
        acc = acc + w_ref[kk:kk + 1, :] * pad_ref[off:off + tm, :]
    a = acc + b_ref[...]
    mu = jnp.mean(a, axis=-1, keepdims=True)
    xc = a - mu
    var = jnp.mean(xc * xc, axis=-1, keepdims=True)
    y = xc * lax.rsqrt(var + EPS) * lg_ref[...] + lb_ref[...]
    o_ref[...] = (y * _sigmoid(y)).astype(o_ref.dtype)


def _short_conv_kernel(seq_info, prev_ref, cur_ref, next_ref, gb_ref, w_ref, o_ref, pad_ref):
    tm = cur_ref.shape[0]
    first, last = _seq_position(pl.program_id(0), *seq_info)
    _fill_padded(pad_ref, prev_ref[...], cur_ref[...], next_ref[...], first, last, tm)
    half = SHORT_KERNEL // 2
    acc = jnp.zeros(cur_ref.shape, F32)
    for kk in range(SHORT_KERNEL):
        off = HALO - half + kk
        acc = acc + w_ref[kk:kk + 1, :] * pad_ref[off:off + tm, :]
    o_ref[...] = (gb_ref[...] * acc).astype(o_ref.dtype)


def _halo_specs(n, tm, width):
    r = tm // HALO
    nh = n // HALO
    return [pl.BlockSpec((HALO, width), lambda i: (jnp.maximum(i * r - 1, 0), 0)),
            pl.BlockSpec((tm, width), lambda i: (i, 0)),
            pl.BlockSpec((HALO, width), lambda i: (jnp.minimum((i + 1) * r, nh - 1), 0))]


def _conv_module(a, w, b, lg, lb, n_ctx, ctx_seq, dec_seq):
    n, width = a.shape
    tm = TOKEN_BLOCK
    seq_info = (n_ctx // tm, ctx_seq // tm, dec_seq // tm)
    full = lambda i: (0, 0)
    return pl.pallas_call(
        functools.partial(_conv_module_kernel, seq_info),
        grid=(n // tm,),
        in_specs=_halo_specs(n, tm, width) + [pl.BlockSpec((CONV_KERNEL, width), full)]
        + [pl.BlockSpec((1, width), full)] * 3,
        out_specs=pl.BlockSpec((tm, width), lambda i: (i, 0)),
        out_shape=jax.ShapeDtypeStruct((n, width), BF16),
        scratch_shapes=[pltpu.VMEM((tm + 2 * HALO, width), F32)],
        compiler_params=_cparams("parallel"),
        name="conv_module",
    )(a, a, a, w, b, lg, lb)


def _short_conv(cv, gb, w, n_ctx, ctx_seq, dec_seq):
    n, width = cv.shape
    tm = TOKEN_BLOCK
    seq_info = (n_ctx // tm, ctx_seq // tm, dec_seq // tm)
    return pl.pallas_call(
        functools.partial(_short_conv_kernel, seq_info),
        grid=(n // tm,),
        in_specs=_halo_specs(n, tm, width) + [pl.BlockSpec((tm, width), lambda i: (i, 0)),
                                              pl.BlockSpec((SHORT_KERNEL, width), lambda i: (0, 0))],
        out_specs=pl.BlockSpec((tm, width), lambda i: (i, 0)),
        out_shape=jax.ShapeDtypeStruct((n, width), BF16),
        scratch_shapes=[pltpu.VMEM((tm + 2 * HALO, width), F32)],
        compiler_params=_cparams("parallel"),
        name="short_conv",
    )(cv, cv, cv, gb, w)


def _attn_heads(q, key_sets, o_ref):
    scale = HEAD_DIM ** -0.5
    grp = N_HEADS // N_KV_HEADS
    for h in range(N_HEADS):
        g = h // grp
        qh = q[:, h * HEAD_DIM:(h + 1) * HEAD_DIM]
        scores = [lax.dot_general(qh, k[:, g * HEAD_DIM:(g + 1) * HEAD_DIM], _NT,
                                  preferred_element_type=F32) * scale for k, _ in key_sets]
        m = scores[0].max(axis=-1, keepdims=True)
        for s in scores[1:]:
            m = jnp.maximum(m, s.max(axis=-1, keepdims=True))
        den = jnp.zeros_like(m)
        num = jnp.zeros((q.shape[0], HEAD_DIM), F32)
        for s, (_, v) in zip(scores, key_sets):
            p = jnp.exp(s - m)
            den = den + p.sum(axis=-1, keepdims=True)
            num = num + jnp.dot(p.astype(BF16), v[:, g * HEAD_DIM:(g + 1) * HEAD_DIM],
                                preferred_element_type=F32)
        o_ref[:, h * HEAD_DIM:(h + 1) * HEAD_DIM] = (num / den).astype(o_ref.dtype)


def _attn_ctx_kernel(q_ref, k_ref, v_ref, o_ref):
    _attn_heads(q_ref[...], [(k_ref[...].astype(BF16), v_ref[...].astype(BF16))], o_ref)


def _attn_lat_kernel(q_ref, ck_ref, cv_ref, k_ref, v_ref, prev_ref, o_ref):
    del prev_ref
    _attn_heads(q_ref[...], [(ck_ref[0, 0].astype(BF16), cv_ref[0, 0].astype(BF16)),
                             (k_ref[...].astype(BF16), v_ref[...].astype(BF16))], o_ref)


def _attention(q, k, v, cache_k, cache_v, layer_j, n_ctx, ctx_seq, dec_seq):
    n = q.shape[0]
    out_shape = jax.ShapeDtypeStruct((n, ATTN_WIDTH), BF16)
    row = lambda b: (b, 0)
    ctx_out = pl.pallas_call(
        _attn_ctx_kernel,
        grid=(n_ctx // ctx_seq,),
        in_specs=[pl.BlockSpec((ctx_seq, ATTN_WIDTH), row),
                  pl.BlockSpec((ctx_seq, KV_WIDTH), row),
                  pl.BlockSpec((ctx_seq, KV_WIDTH), row)],
        out_specs=pl.BlockSpec((ctx_seq, ATTN_WIDTH), row),
        out_shape=out_shape,
        compiler_params=_cparams("parallel"),
        name="attn_ctx",
    )(q, k, v)
    tq = TOKEN_BLOCK
    dec_batch = (n - n_ctx) // dec_seq
    past = cache_k.shape[2]
    qidx = lambda b, t: (n_ctx // tq + b * (dec_seq // tq) + t, 0)
    kidx = lambda b, t: (n_ctx // dec_seq + b, 0)
    cidx = lambda b, t: (b, layer_j, 0, 0)
    return pl.pallas_call(
        _attn_lat_kernel,
        grid=(dec_batch, dec_seq // tq),
        in_specs=[pl.BlockSpec((tq, ATTN_WIDTH), qidx),
                  pl.BlockSpec((1, 1, past, KV_WIDTH), cidx),
                  pl.BlockSpec((1, 1, past, KV_WIDTH), cidx),
                  pl.BlockSpec((dec_seq, KV_WIDTH), kidx),
                  pl.BlockSpec((dec_seq, KV_WIDTH), kidx),
                  pl.BlockSpec(memory_space=pl.ANY)],
        out_specs=pl.BlockSpec((tq, ATTN_WIDTH), qidx),
        out_shape=out_shape,
        input_output_aliases={5: 0},
        compiler_params=_cparams("parallel", "arbitrary"),
        name="attn_lat",
    )(q, cache_k, cache_v, k, v, ctx_out)


def _out_proj_kernel(x_ref, mod_ref, y1_ref, y2_ref, w1_ref, w2_ref, o_ref):
    y = (jnp.dot(y1_ref[...], w1_ref[...], preferred_element_type=F32)
         + jnp.dot(y2_ref[...], w2_ref[...], preferred_element_type=F32))
    o_ref[...] = x_ref[...] + mod_ref[0, 2:3, :] * y


def _out_proj(x, mod, y1, y2, w_bf, n_ctx, dec_seq):
    n, d = x.shape
    tm = TOKEN_BLOCK
    ncb, lb = n_ctx // tm, dec_seq // tm
    w1, w2 = y1.shape[1], y2.shape[1]
    row = lambda i: (i, 0)
    return pl.pallas_call(
        _out_proj_kernel,
        grid=(n // tm,),
        in_specs=[pl.BlockSpec((tm, d), row),
                  pl.BlockSpec((1, N_MOD, d), lambda i: (_group_of_block(i, ncb, lb), 0, 0)),
                  pl.BlockSpec((tm, w1), row),
                  pl.BlockSpec((tm, w2), row),
                  pl.BlockSpec((w1, d), lambda i: (0, 0)),
                  pl.BlockSpec((w2, d), lambda i: (1, 0))],
        out_specs=pl.BlockSpec((tm, d), row),
        out_shape=jax.ShapeDtypeStruct((n, d), F32),
        compiler_params=_cparams("parallel"),
        name="out_proj",
    )(x, mod, y1, y2, w_bf, w_bf)


def _odd_in_kernel(x_ref, mod_ref, g_ref, w_ref, f_ref, gb_ref, cv_ref):
    h = _modulate(x_ref[...], g_ref[...], mod_ref[0, 0:1, :], mod_ref[0, 1:2, :])
    proj = jnp.dot(h.astype(BF16), w_ref[...], preferred_element_type=F32)
    fw, sw = FOURIER_WIDTH, SHORT_WIDTH
    f_ref[...] = proj[:, :fw].astype(BF16)
    gb_ref[...] = proj[:, fw:fw + sw]
    cv_ref[...] = proj[:, fw + sw:fw + 2 * sw] * proj[:, fw + 2 * sw:]


def _odd_in(x, mod, g, w_bf, n_ctx, dec_seq):
    n, d = x.shape
    tm = TOKEN_BLOCK
    ncb, lb = n_ctx // tm, dec_seq // tm
    row = lambda i: (i, 0)
    full = lambda i: (0, 0)
    return pl.pallas_call(
        _odd_in_kernel,
        grid=(n // tm,),
        in_specs=[pl.BlockSpec((tm, d), row),
                  pl.BlockSpec((1, N_MOD, d), lambda i: (_group_of_block(i, ncb, lb), 0, 0)),
                  pl.BlockSpec((1, d), full),
                  pl.BlockSpec((d, w_bf.shape[1]), full)],
        out_specs=[pl.BlockSpec((tm, FOURIER_WIDTH), row),
                   pl.BlockSpec((tm, SHORT_WIDTH), row),
                   pl.BlockSpec((tm, SHORT_WIDTH), row)],
        out_shape=[jax.ShapeDtypeStruct((n, FOURIER_WIDTH), BF16),
                   jax.ShapeDtypeStruct((n, SHORT_WIDTH), F32),
                   jax.ShapeDtypeStruct((n, SHORT_WIDTH), F32)],
        compiler_params=_cparams("parallel"),
        name="odd_in",
    )(x, mod, g, w_bf)


def _dft_table_kernel(ac_ref, as_ref, bc_ref, bs_ref, c_ref, s_ref):
    ac = ac_ref[0]
    a_s = as_ref[0]
    bc = bc_ref[...]
    bs = bs_ref[...]
    c_ref[...] = (ac * bc - a_s * bs).astype(BF16)
    s_ref[...] = (a_s * bc + ac * bs).astype(BF16)


def _dft_tables(t):
    fine = 64
    coarse = t // fine
    k = np.arange(t, dtype=np.int64)
    ang_a = 2.0 * np.pi * ((np.arange(coarse, dtype=np.int64)[:, None] * fine * k[None, :]) % t) / t
    ang_b = 2.0 * np.pi * ((np.arange(fine, dtype=np.int64)[:, None] * k[None, :]) % t) / t
    ac = jnp.asarray(np.cos(ang_a), F32).reshape(coarse, 1, t)
    a_s = jnp.asarray(np.sin(ang_a), F32).reshape(coarse, 1, t)
    bc = jnp.asarray(np.cos(ang_b), F32)
    bs = jnp.asarray(np.sin(ang_b), F32)
    a_spec = pl.BlockSpec((1, 1, t), lambda i: (i, 0, 0))
    b_spec = pl.BlockSpec((fine, t), lambda i: (0, 0))
    o_spec = pl.BlockSpec((fine, t), lambda i: (i, 0))
    return pl.pallas_call(
        _dft_table_kernel,
        grid=(coarse,),
        in_specs=[a_spec, a_spec, b_spec, b_spec],
        out_specs=[o_spec, o_spec],
        out_shape=[jax.ShapeDtypeStruct((t, t), BF16)] * 2,
        compiler_params=_cparams("parallel"),
        name="dft_tables",
    )(ac, a_s, bc, bs)


def _small_dft(n):
    jk = (np.arange(n, dtype=np.int64)[:, None] * np.arange(n, dtype=np.int64)[None, :]) % n
    ang = 2.0 * np.pi * jk / n
    return np.cos(ang), np.sin(ang)


def _channel_dft(f, cb, sb):
    xc = jnp.dot(f, cb, preferred_element_type=F32).astype(BF16)
    xs = jnp.dot(f, sb, preferred_element_type=F32).astype(BF16)
    return xc, xs


def _fourier_ctx_kernel(scale, f_ref, cb_ref, sb_ref, ct_ref, st_ref, o_ref):
    xc, xs = _channel_dft(f_ref[...], cb_ref[...], sb_ref[...])
    y = (jnp.dot(ct_ref[...], xc, preferred_element_type=F32)
         - jnp.dot(st_ref[...], xs, preferred_element_type=F32))
    o_ref[...] = (y * scale).astype(o_ref.dtype)


def _fourier_lat_kernel(scale, f_ref, cb_ref, sb_ref, ct_ref, st_ref, prev_ref, o_ref, xc_ref, xs_ref):
    del prev_ref

    @pl.when(pl.program_id(1) == 0)
    def _():
        xc, xs = _channel_dft(f_ref[...], cb_ref[...], sb_ref[...])
        xc_ref[...] = xc
        xs_ref[...] = xs

    y = (jnp.dot(ct_ref[...], xc_ref[...], preferred_element_type=F32)
         - jnp.dot(st_ref[...], xs_ref[...], preferred_element_type=F32))
    o_ref[...] = (y * scale).astype(o_ref.dtype)


def _fourier(f, cb, sb, ct_ctx, st_ctx, ct_lat, st_lat, n_ctx, ctx_seq, dec_seq):
    n, width = f.shape
    out_shape = jax.ShapeDtypeStruct((n, width), BF16)
    full = lambda *_: (0, 0)
    ctx_out = pl.pallas_call(
        functools.partial(_fourier_ctx_kernel, 1.0 / math.sqrt(ctx_seq * FOURIER_GROUP_DIM)),
        grid=(n_ctx // ctx_seq,),
        in_specs=[pl.BlockSpec((ctx_seq, width), lambda b: (b, 0)),
                  pl.BlockSpec((width, width), full),
                  pl.BlockSpec((width, width), full),
                  pl.BlockSpec((ctx_seq, ctx_seq), full),
                  pl.BlockSpec((ctx_seq, ctx_seq), full)],
        out_specs=pl.BlockSpec((ctx_seq, width), lambda b: (b, 0)),
        out_shape=out_shape,
        compiler_params=_cparams("parallel"),
        name="fourier_ctx",
    )(f, cb, sb, ct_ctx, st_ctx)
    tr = TOKEN_BLOCK
    dec_batch = (n - n_ctx) // dec_seq
    fidx = lambda b, r: (n_ctx // dec_seq + b, 0)
    oidx = lambda b, r: (n_ctx // tr + b * (dec_seq // tr) + r, 0)
    return pl.pallas_call(
        functools.partial(_fourier_lat_kernel, 1.0 / math.sqrt(dec_seq * FOURIER_GROUP_DIM)),
        grid=(dec_batch, dec_seq // tr),
        in_specs=[pl.BlockSpec((dec_seq, width), fidx),
                  pl.BlockSpec((width, width), full),
                  pl.BlockSpec((width, width), full),
                  pl.BlockSpec((tr, dec_seq), lambda b, r: (r, 0)),
                  pl.BlockSpec((tr, dec_seq), lambda b, r: (r, 0)),
                  pl.BlockSpec(memory_space=pl.ANY)],
        out_specs=pl.BlockSpec((tr, width), oidx),
        out_shape=out_shape,
        scratch_shapes=[pltpu.VMEM((dec_seq, width), BF16)] * 2,
        input_output_aliases={5: 0},
        compiler_params=_cparams("parallel", "arbitrary"),
        name="fourier_lat",
    )(f, cb, sb, ct_lat, st_lat, ctx_out)


K1 = PEER_TOPK + 1
KEY_VREGS = PEER_KEYS // SUBLANES
LIST_VREGS = -(-K1 // SUBLANES)


def _oddeven_merge_sort_pairs(n):
    pairs = []
    p = 1
    while p < n:
        k = p
        while k >= 1:
            for j in range(k % p, n - k, 2 * k):
                for i in range(min(k, n - j - k)):
                    if (i + j) // (2 * p) == (i + j + k) // (2 * p):
                        pairs.append((i + j, i + j + k))
            k //= 2
        p *= 2
    return pairs


_SORT_PAIRS = _oddeven_merge_sort_pairs(KEY_VREGS)


def _all_sublanes(x, op):
    for shift in (4, 2, 1):
        x = op(x, pltpu.roll(x, shift, 0))
    return x


def _row_iota():
    return lax.broadcasted_iota(jnp.int32, (SUBLANES, LANES), 0).astype(F32)


def _sorted_top(tiles):
    v = list(tiles)
    for i, j in _SORT_PAIRS:
        v[i], v[j] = jnp.maximum(v[i], v[j]), jnp.minimum(v[i], v[j])
    row8 = _row_iota()
    cnt = jnp.zeros((SUBLANES, LANES), F32)
    outs = [jnp.full((SUBLANES, LANES), NEG_INF, F32) for _ in range(LIST_VREGS)]
    for r in range(K1):
        depth = min(KEY_VREGS, K1 - r)
        m = _all_sublanes(v[0], jnp.maximum)
        eq = v[0] == m
        new = cnt + _all_sublanes(jnp.where(eq, 1.0, 0.0), jnp.add)
        for k in range(LIST_VREGS):
            if SUBLANES * k + SUBLANES - 1 >= r:
                rowk = row8 + float(SUBLANES * k)
                outs[k] = jnp.where(rowk >= cnt, jnp.where(rowk < new, m, outs[k]), outs[k])
        for d in range(depth - 1):
            v[d] = jnp.where(eq, v[d + 1], v[d])
        v[depth - 1] = jnp.where(eq, NEG_INF, v[depth - 1])
        cnt = new
    last = LIST_VREGS - 1
    outs[last] = jnp.where(row8 + float(SUBLANES * last) < float(K1), outs[last], NEG_INF)
    return outs


def _list_entry(lst, q):
    return jnp.broadcast_to(lst[q // SUBLANES][q % SUBLANES:q % SUBLANES + 1, :], (SUBLANES, LANES))


def _threshold_and_norm(a, b_vals):
    row8 = _row_iota()
    stack = [jnp.where(row8 < float(K1 // (q + 1)), a[0] + b_vals[q], NEG_INF) for q in range(K1)]
    rest = [a[k] + b_vals[0] for k in range(1, LIST_VREGS)]
    cmax = _list_entry(a, 0) + b_vals[0]
    cnt = jnp.zeros((SUBLANES, LANES), F32)
    hi = jnp.full((SUBLANES, LANES), NEG_INF, F32)
    lo = jnp.full((SUBLANES, LANES), NEG_INF, F32)
    z = jnp.zeros((SUBLANES, LANES), F32)
    for r in range(K1):
        depth = K1 - r
        top = stack[0]
        for t in rest:
            top = jnp.maximum(top, t)
        m = _all_sublanes(top, jnp.maximum)
        eq0 = stack[0] == m
        ones = jnp.where(eq0, 1.0, 0.0)
        for t in rest:
            ones = ones + jnp.where(t == m, 1.0, 0.0)
        c = _all_sublanes(ones, jnp.add)
        new = cnt + c
        z = z + jnp.where(cnt < float(PEER_TOPK), c * jnp.exp(m - cmax), 0.0)
        hi = jnp.where(cnt < float(PEER_TOPK), jnp.where(new >= float(PEER_TOPK), m, hi), hi)
        lo = jnp.where(cnt < float(K1), jnp.where(new >= float(K1), m, lo), lo)
        for d in range(depth - 1):
            stack[d] = jnp.where(eq0, stack[d + 1], stack[d])
        stack[depth - 1] = jnp.where(eq0, NEG_INF, stack[depth - 1])
        rest = [jnp.where(t == m, NEG_INF, t) for t in rest]
        cnt = new
    return 0.5 * (hi + lo), z


def _select_tile(s1, s2):
    a = _sorted_top(s1)
    b = _sorted_top(s2)
    b_vals = [_list_entry(b, q) for q in range(K1)]
    tmid, z = _threshold_and_norm(a, b_vals)
    a0 = _list_entry(a, 0)
    inv_z = 1.0 / z
    counts, ranks, e1, e2 = [], [], [], []
    for t1, t2 in zip(s1, s2):
        thr = tmid - t1
        cnt = jnp.zeros_like(t1)
        rank = jnp.zeros_like(t2)
        for q in range(PEER_TOPK):
            cnt = jnp.where(b_vals[q] >= thr, float(q + 1), cnt)
            rank = jnp.where(b_vals[q] > t2, float(q + 1), rank)
        counts.append(cnt)
        ranks.append(rank)
        e1.append(jnp.exp(t1 - a0) * inv_z)
        e2.append(jnp.exp(t2 - b_vals[0]))
    return counts, ranks, e1, e2


def _pack_pairs(tiles):
    words = [pltpu.bitcast(jnp.concatenate(tiles[c:c + 2], axis=0).astype(BF16), jnp.uint32)
             for c in range(0, len(tiles), 2)]
    return jnp.concatenate(words, axis=0)


def _peer_select_kernel(x_ref, mod_ref, g_ref, wq_ref, keys_ref, h_ref, cnt_ref, e1_ref, rank_ref, e2_ref, q_scr):
    h2 = _modulate(x_ref[...], g_ref[...], mod_ref[0, 3:4, :], mod_ref[0, 4:5, :]).astype(BF16)
    h_ref[...] = h2
    q = jnp.dot(h2, wq_ref[...], preferred_element_type=F32).astype(BF16)
    for c in range(2 * PEER_HEADS):
        q_scr[c] = q[:, c * PEER_HALF:(c + 1) * PEER_HALF]

    def head(h, carry):
        s1 = lax.dot_general(keys_ref[2 * h], q_scr[2 * h], _NT, preferred_element_type=F32)
        s2 = lax.dot_general(keys_ref[2 * h + 1], q_scr[2 * h + 1], _NT, preferred_element_type=F32)
        for lt in range(s1.shape[1] // LANES):
            lanes = slice(lt * LANES, (lt + 1) * LANES)
            tiles = lambda s: [s[SUBLANES * v:SUBLANES * (v + 1), lanes] for v in range(KEY_VREGS)]
            counts, ranks, e1, e2 = _select_tile(tiles(s1), tiles(s2))
            cnt_ref[h, :, lanes] = jnp.concatenate(counts, axis=0)
            e1_ref[h, :, lanes] = jnp.concatenate(e1, axis=0)
            rank_ref[h, :, lanes] = _pack_pairs(ranks)
            e2_ref[h, :, lanes] = _pack_pairs(e2)
        return carry

    lax.fori_loop(0, PEER_HEADS, head, 0)


def _peer_select(x, mod, g, wq_bf, keys_bf, n_ctx, dec_seq):
    n, d = x.shape
    tm = TOKEN_BLOCK
    ncb, lb = n_ctx // tm, dec_seq // tm
    row = lambda i: (i, 0)
    sel_spec = pl.BlockSpec((PEER_HEADS, PEER_KEYS, tm), lambda i: (0, 0, i))
    sel_f32 = jax.ShapeDtypeStruct((PEER_HEADS, PEER_KEYS, n), F32)
    packed = jax.ShapeDtypeStruct((PEER_HEADS, PEER_KEYS // 2, n), jnp.uint32)
    packed_spec = pl.BlockSpec((PEER_HEADS, PEER_KEYS // 2, tm), lambda i: (0, 0, i))
    return pl.pallas_call(
        _peer_select_kernel,
        grid=(n // tm,),
        in_specs=[pl.BlockSpec((tm, d), row),
                  pl.BlockSpec((1, N_MOD, d), lambda i: (_group_of_block(i, ncb, lb), 0, 0)),
                  pl.BlockSpec((1, d), lambda i: (0, 0)),
                  pl.BlockSpec(wq_bf.shape, lambda i: (0, 0)),
                  pl.BlockSpec(keys_bf.shape, lambda i: (0, 0, 0))],
        out_specs=[pl.BlockSpec((tm, d), row), sel_spec, sel_spec, packed_spec, packed_spec],
        out_shape=[jax.ShapeDtypeStruct((n, d), BF16), sel_f32, sel_f32, packed, packed],
        scratch_shapes=[pltpu.VMEM((2 * PEER_HEADS, tm, PEER_HALF), BF16)],
        compiler_params=_cparams("parallel"),
        name="peer_select",
    )(x, mod, g, wq_bf, keys_bf)


def _gelu_tanh(x):
    c = math.sqrt(2.0 / math.pi)
    return x * (0.5 * (1.0 + jnp.tanh(c * (x + 0.044715 * (x * x * x)))))


PACKED_ROWS = 2 * SUBLANES


def _peer_dense_kernel(x_ref, mod_ref, h_ref, u_ref, vt_ref, cnt_ref, e1_ref, rank_ref, e2_ref, o_ref,
                       acc_ref, act_ref, gt_ref):
    eb = pl.program_id(1)

    @pl.when(eb == 0)
    def _():
        acc_ref[...] = jnp.zeros_like(acc_ref)

    h2 = h_ref[...]
    tb = h2.shape[0]
    n_sub = DENSE_EXPERTS // DENSE_SUB
    keys_per_sub = DENSE_SUB // PEER_KEYS
    zero = jnp.zeros((), BF16)

    def first_matmul(sb):
        rows = slice(sb * DENSE_SUB, (sb + 1) * DENSE_SUB)
        act_ref[rows, :] = lax.dot_general(u_ref[rows, :], h2, _NT, preferred_element_type=F32)

    def weights(sb):
        for il in range(keys_per_sub):
            ik = sb * keys_per_sub + il
            for lt in range(tb // LANES):
                lanes = slice(lt * LANES, (lt + 1) * LANES)
                bcast = lambda ref, h: jnp.broadcast_to(ref[h, ik:ik + 1, lanes], (PACKED_ROWS, LANES)).astype(BF16)
                cnt = [bcast(cnt_ref, h) for h in range(PEER_HEADS)]
                e1 = [bcast(e1_ref, h) for h in range(PEER_HEADS)]
                for rc in range(PEER_KEYS // PACKED_ROWS):
                    jr = slice(rc * SUBLANES, (rc + 1) * SUBLANES)
                    unpack = lambda ref, h: pltpu.bitcast(ref[h, jr, lanes], BF16)
                    w = None
                    for h in range(PEER_HEADS):
                        term = jnp.where(unpack(rank_ref, h) < cnt[h], unpack(e2_ref, h), zero) * e1[h]
                        w = term if w is None else w + term
                    er = slice(ik * PEER_KEYS + rc * PACKED_ROWS, ik * PEER_KEYS + (rc + 1) * PACKED_ROWS)
                    gt_ref[er, lanes] = w * _gelu_tanh(act_ref[er, lanes]).astype(BF16)

    tot = None
    first_matmul(0)
    for sb in range(n_sub):
        if sb + 1 < n_sub:
            first_matmul(sb + 1)
        weights(sb)
        rows = slice(sb * DENSE_SUB, (sb + 1) * DENSE_SUB)
        part = jnp.dot(vt_ref[:, rows], gt_ref[rows, :], preferred_element_type=F32)
        tot = part if tot is None else tot + part
    acc_ref[...] += tot

    @pl.when(eb == pl.num_programs(1) - 1)
    def _():
        o_ref[...] = x_ref[...] + mod_ref[0, 5:6, :] * acc_ref[...].T


def _peer_dense(x, mod, h2, u_bf, vt_bf, cnt, e1, rank, e2, n_ctx, dec_seq):
    n, d = x.shape
    tb, eb = DENSE_TOKENS, DENSE_EXPERTS
    n_exp = u_bf.shape[0]
    ncb, lb = n_ctx // tb, dec_seq // tb
    row = lambda t, e: (t, 0)
    key_spec = pl.BlockSpec((PEER_HEADS, eb // PEER_KEYS, tb), lambda t, e: (0, e, t))
    all_spec = pl.BlockSpec((PEER_HEADS, PEER_KEYS // 2, tb), lambda t, e: (0, 0, t))
    return pl.pallas_call(
        _peer_dense_kernel,
        grid=(n // tb, n_exp // eb),
        in_specs=[pl.BlockSpec((tb, d), row),
                  pl.BlockSpec((1, N_MOD, d), lambda t, e: (_group_of_block(t, ncb, lb), 0, 0)),
                  pl.BlockSpec((tb, d), row),
                  pl.BlockSpec((eb, d), lambda t, e: (e, 0)),
                  pl.BlockSpec((d, eb), lambda t, e: (0, e)),
                  key_spec, key_spec, all_spec, all_spec],
        out_specs=pl.BlockSpec((tb, d), row),
        out_shape=jax.ShapeDtypeStruct((n, d), F32),
        scratch_shapes=[pltpu.VMEM((d, tb), F32),
                        pltpu.VMEM((eb, tb), F32),
                        pltpu.VMEM((eb, tb), BF16)],
        compiler_params=_cparams("parallel", "arbitrary"),
        name="peer_dense",
    )(x, mod, h2, u_bf, vt_bf, cnt, e1, rank, e2)


def _final_norm_kernel(x_ref, g_ref, o_ref):
    x = x_ref[...]
    ms = jnp.mean(x * x, axis=-1, keepdims=True)
    o_ref[...] = x * lax.rsqrt(ms + EPS) * g_ref[...]


def _final_norm(x, g):
    n, d = x.shape
    tm = TOKEN_BLOCK
    return pl.pallas_call(
        _final_norm_kernel,
        grid=(n // tm,),
        in_specs=[pl.BlockSpec((tm, d), lambda i: (i, 0)), pl.BlockSpec((1, d), lambda i: (0, 0))],
        out_specs=pl.BlockSpec((tm, d), lambda i: (i, 0)),
        out_shape=jax.ShapeDtypeStruct((n, d), F32),
        compiler_params=_cparams("parallel"),
        name="final_norm",
    )(x, g)


def _rope_tables(n_tokens):
    rows = n_tokens // GRID_W
    row = np.repeat(np.arange(rows), GRID_W).astype(np.float32)
    col = np.tile(np.arange(GRID_W), rows).astype(np.float32)
    half = HEAD_DIM // 2
    inv = (np.float32(ROPE_THETA) ** (-np.arange(0, half, 2, dtype=np.float32) / np.float32(half))).astype(np.float32)
    ang = np.concatenate([row[:, None] * inv, col[:, None] * inv], axis=-1).astype(np.float32)
    cos = np.repeat(np.cos(ang.astype(np.float64)), 2, axis=-1)
    sin = np.repeat(np.sin(ang.astype(np.float64)), 2, axis=-1)
    sin = sin * np.tile(np.array([-1.0, 1.0]), HEAD_DIM // 2)
    return (jnp.asarray(np.tile(cos, (1, N_HEADS)), F32), jnp.asarray(np.tile(sin, (1, N_HEADS)), F32))


def _block_diag(block, count):
    n = block.shape[0]
    out = np.zeros((n * count, n * count), np.float64)
    for c in range(count):
        out[c * n:(c + 1) * n, c * n:(c + 1) * n] = block
    return out


def kernel(x_prompt, x_sample, cache_k, cache_v, c, c_ctx, w_mod, b_mod, norm_mix_g, norm_ffn_g, w_in_even, w_out_even, conv_dw_w, conv_dw_b, conv_ln_g, conv_ln_b, q_norm_g, k_norm_g, w_in_odd, w_out_odd, short_conv_w, peer_wq, peer_keys, peer_u, peer_v, final_norm_g):
    batch, ctx_seq, d = x_prompt.shape
    dec_batch, dec_seq, _ = x_sample.shape
    depth = w_mod.shape[0]
    n_ctx = batch * ctx_seq
    past = cache_k.shape[2]

    x = jnp.concatenate([x_prompt.reshape(n_ctx, d), x_sample.reshape(dec_batch * dec_seq, d)], axis=0)
    mods = _modulation(jnp.concatenate([c_ctx[None, :], c], axis=0), w_mod, b_mod)

    cos, sin = _rope_tables(dec_seq)
    bd = jnp.asarray(_block_diag(np.ones((HEAD_DIM, HEAD_DIM)), N_HEADS), BF16)
    c128, s128 = _small_dft(FOURIER_GROUP_DIM)
    cb = jnp.asarray(_block_diag(c128, FOURIER_GROUPS), BF16)
    sb = jnp.asarray(_block_diag(s128, FOURIER_GROUPS), BF16)
    c_ctx_t, s_ctx_t = _small_dft(ctx_seq)
    ct_ctx, st_ctx = jnp.asarray(c_ctx_t, BF16), jnp.asarray(s_ctx_t, BF16)
    if depth > 1:
        ct_lat, st_lat = _dft_tables(dec_seq)
    ck = cache_k.reshape(dec_batch, -1, past, KV_WIDTH)
    cv = cache_v.reshape(dec_batch, -1, past, KV_WIDTH)

    new_k, new_v = [], []
    for l in range(depth):
        mod = mods[l]
        j = l // 2
        g_mix = norm_mix_g[l][None, :]
        if l % 2 == 0:
            qg = jnp.tile(q_norm_g[j], N_HEADS)[None, :]
            kg = jnp.tile(k_norm_g[j], N_KV_HEADS)[None, :]
            a, q, k, v = _even_in(x, mod, g_mix, w_in_even[j].astype(BF16), qg, kg, bd, cos, sin, n_ctx, dec_seq)
            new_k.append(k[:n_ctx].reshape(batch, ctx_seq, N_KV_HEADS, HEAD_DIM))
            new_v.append(v[:n_ctx].reshape(batch, ctx_seq, N_KV_HEADS, HEAD_DIM))
            y1 = _conv_module(a, conv_dw_w[j], conv_dw_b[j][None, :], conv_ln_g[j][None, :],
                              conv_ln_b[j][None, :], n_ctx, ctx_seq, dec_seq)
            y2 = _attention(q, k, v, ck, cv, j, n_ctx, ctx_seq, dec_seq)
            x = _out_proj(x, mod, y1, y2, w_out_even[j].astype(BF16), n_ctx, dec_seq)
        else:
            f, gb, cvv = _odd_in(x, mod, g_mix, w_in_odd[j].astype(BF16), n_ctx, dec_seq)
            y1 = _fourier(f, cb, sb, ct_ctx, st_ctx, ct_lat, st_lat, n_ctx, ctx_seq, dec_seq)
            y2 = _short_conv(cvv, gb, short_conv_w[j], n_ctx, ctx_seq, dec_seq)
            x = _out_proj(x, mod, y1, y2, w_out_odd[j].astype(BF16), n_ctx, dec_seq)
        keys_bf = peer_keys[l].reshape(2 * PEER_HEADS, PEER_KEYS, PEER_HALF).astype(BF16)
        h2, cnt, e1, rank, e2 = _peer_select(x, mod, norm_ffn_g[l][None, :], peer_wq[l].astype(BF16), keys_bf,
                                             n_ctx, dec_seq)
        x = _peer_dense(x, mod, h2, peer_u[l].astype(BF16), peer_v[l].T.astype(BF16), cnt, e1, rank, e2,
                        n_ctx, dec_seq)

    y = _final_norm(x, final_norm_g[None, :])
    y_prompt = y[:n_ctx].reshape(batch, ctx_seq, d)
    y_sample = y[n_ctx:].reshape(dec_batch, dec_seq, d)
    return (y_prompt, y_sample, jnp.stack(new_k, axis=1), jnp.stack(new_v, axis=1))
```

```python
import functools
import math

import numpy as np
import jax
import jax.numpy as jnp
from jax import lax
from jax.experimental import pallas as pl
from jax.experimental.pallas import tpu as pltpu

F32 = jnp.float32
BF16 = jnp.bfloat16
EPS = 1e-6
NEG_INF = float("-inf")

GRID_W = 64
ROPE_THETA = 10000.0
N_MOD = 6
CONV_WIDTH = 512
CONV_KERNEL = 31
N_HEADS = 8
N_KV_HEADS = 2
HEAD_DIM = 64
ATTN_WIDTH = N_HEADS * HEAD_DIM
KV_WIDTH = N_KV_HEADS * HEAD_DIM
FOURIER_GROUPS = 4
FOURIER_GROUP_DIM = 128
FOURIER_WIDTH = FOURIER_GROUPS * FOURIER_GROUP_DIM
SHORT_WIDTH = 512
SHORT_KERNEL = 3
PEER_HEADS = 8
PEER_KEYS = 128
PEER_HALF = 128
PEER_TOPK = 16

LANES = 128
SUBLANES = 8
HALO = 16
TOKEN_BLOCK = 256
DENSE_TOKENS = 512
DENSE_EXPERTS = 1024
VMEM_LIMIT = 56 * 1024 * 1024

_NT = (((1,), (1,)), ((), ()))


def _cparams(*sem, vmem=VMEM_LIMIT):
    return pltpu.CompilerParams(dimension_semantics=sem, vmem_limit_bytes=vmem)


def _sigmoid(x):
    return 1.0 / (1.0 + jnp.exp(-x))


def _modulate(x, g, shift, scale):
    ms = jnp.mean(x * x, axis=-1, keepdims=True)
    y = x * lax.rsqrt(ms + EPS) * g
    return y * (1.0 + scale) + shift


def _group_of_block(i, n_ctx_blocks, blocks_per_latent):
    return jnp.where(i < n_ctx_blocks, 0, 1 + (i - n_ctx_blocks) // blocks_per_latent)


def _mod_kernel(cond_ref, w_ref, b_ref, o_ref):
    c = cond_ref[...]
    s = c * _sigmoid(c)
    o_ref[0] = jnp.dot(s, w_ref[0], preferred_element_type=F32) + b_ref[0]


def _modulation(cond, w_mod, b_mod):
    n_layers, d, f = w_mod.shape
    g = cond.shape[0]
    gp = -(-g // SUBLANES) * SUBLANES
    cond_p = jnp.zeros((gp, d), F32).at[:g].set(cond)
    cb = 1536
    out = pl.pallas_call(
        _mod_kernel,
        grid=(n_layers, f // cb),
        in_specs=[pl.BlockSpec((gp, d), lambda l, j: (0, 0)),
                  pl.BlockSpec((1, d, cb), lambda l, j: (l, 0, j)),
                  pl.BlockSpec((1, 1, cb), lambda l, j: (l, 0, j))],
        out_specs=pl.BlockSpec((1, gp, cb), lambda l, j: (l, 0, j)),
        out_shape=jax.ShapeDtypeStruct((n_layers, gp, f), F32),
        compiler_params=_cparams("parallel", "arbitrary"),
        name="modulation",
    )(cond_p, w_mod, b_mod.reshape(n_layers, 1, f))
    return out[:, :g].reshape(n_layers, g, N_MOD, d)


def _head_norm(t, bd, gain):
    tt = t * t
    hi = tt.astype(BF16)
    lo = (tt - hi.astype(F32)).astype(BF16)
    ss = (jnp.dot(hi, bd, preferred_element_type=F32) + jnp.dot(lo, bd, preferred_element_type=F32))
    return t * lax.rsqrt(ss * (1.0 / HEAD_DIM) + EPS) * gain


def _pair_swap(x):
    n = x.shape[-1]
    lane = lax.broadcasted_iota(jnp.int32, x.shape, 1)
    return jnp.where((lane & 1) == 0, pltpu.roll(x, n - 1, 1), pltpu.roll(x, 1, 1))


def _even_in_kernel(n_ctx_blocks, x_ref, mod_ref, g_ref, w_ref, qg_ref, kg_ref, bd_ref, cos_ref, sin_ref,
                    a_ref, q_ref, k_ref, v_ref):
    i = pl.program_id(0)
    h = _modulate(x_ref[...], g_ref[...], mod_ref[0, 0:1, :], mod_ref[0, 1:2, :])
    proj = jnp.dot(h.astype(BF16), w_ref[...], preferred_element_type=F32)
    s1, s2, s3, s4 = CONV_WIDTH, 2 * CONV_WIDTH, 2 * CONV_WIDTH + ATTN_WIDTH, 2 * CONV_WIDTH + ATTN_WIDTH + KV_WIDTH
    a_ref[...] = proj[:, :s1] * _sigmoid(proj[:, s1:s2])
    bd = bd_ref[...]
    qn = _head_norm(proj[:, s2:s3], bd, qg_ref[...])
    kn = _head_norm(proj[:, s3:s4], bd[:KV_WIDTH, :KV_WIDTH], kg_ref[...])
    v_ref[...] = proj[:, s4:]

    @pl.when(i < n_ctx_blocks)
    def _():
        q_ref[...] = qn.astype(BF16)
        k_ref[...] = kn

    @pl.when(i >= n_ctx_blocks)
    def _():
        cos = cos_ref[...]
        sin = sin_ref[...]
        q_ref[...] = (qn * cos + _pair_swap(qn) * sin).astype(BF16)
        k_ref[...] = kn * cos[:, :KV_WIDTH] + _pair_swap(kn) * sin[:, :KV_WIDTH]


def _even_in(x, mod, g, w_bf, qg, kg, bd, cos, sin, n_ctx, dec_seq):
    n, d = x.shape
    tm = TOKEN_BLOCK
    ncb, lb = n_ctx // tm, dec_seq // tm
    grp = lambda i: (_group_of_block(i, ncb, lb), 0, 0)
    rope_idx = lambda i: (jnp.maximum(i - ncb, 0) % lb, 0)
    full = lambda i: (0, 0)
    row = lambda i: (i, 0)
    f_in = w_bf.shape[1]
    return pl.pallas_call(
        functools.partial(_even_in_kernel, ncb),
        grid=(n // tm,),
        in_specs=[pl.BlockSpec((tm, d), row),
                  pl.BlockSpec((1, N_MOD, d), grp),
                  pl.BlockSpec((1, d), full),
                  pl.BlockSpec((d, f_in), full),
                  pl.BlockSpec((1, ATTN_WIDTH), full),
                  pl.BlockSpec((1, KV_WIDTH), full),
                  pl.BlockSpec((ATTN_WIDTH, ATTN_WIDTH), full),
                  pl.BlockSpec((tm, ATTN_WIDTH), rope_idx),
                  pl.BlockSpec((tm, ATTN_WIDTH), rope_idx)],
        out_specs=[pl.BlockSpec((tm, CONV_WIDTH), row),
                   pl.BlockSpec((tm, ATTN_WIDTH), row),
                   pl.BlockSpec((tm, KV_WIDTH), row),
                   pl.BlockSpec((tm, KV_WIDTH), row)],
        out_shape=[jax.ShapeDtypeStruct((n, CONV_WIDTH), F32),
                   jax.ShapeDtypeStruct((n, ATTN_WIDTH), BF16),
                   jax.ShapeDtypeStruct((n, KV_WIDTH), F32),
                   jax.ShapeDtypeStruct((n, KV_WIDTH), F32)],
        compiler_params=_cparams("parallel"),
        name="even_in",
    )(x, mod, g, w_bf, qg, kg, bd, cos, sin)


def _seq_position(i, n_ctx_blocks, ctx_seq_blocks, lat_seq_blocks):
    is_ctx = i < n_ctx_blocks
    sb = jnp.where(is_ctx, ctx_seq_blocks, lat_seq_blocks)
    pos = jnp.where(is_ctx, i, i - n_ctx_blocks) % sb
    return pos == 0, pos == sb - 1


def _fill_padded(pad_ref, prev, cur, nxt, first, last, tm):
    pad_ref[0:HALO, :] = jnp.where(first, 0.0, prev)
    pad_ref[HALO:HALO + tm, :] = cur
    pad_ref[HALO + tm:2 * HALO + tm, :] = jnp.where(last, 0.0, nxt)


def _conv_module_kernel(seq_info, prev_ref, cur_ref, next_ref, w_ref, b_ref, lg_ref, lb_ref, o_ref, pad_ref):
    tm = cur_ref.shape[0]
    first, last = _seq_position(pl.program_id(0), *seq_info)
    _fill_padded(pad_ref, prev_ref[...], cur_ref[...], next_ref[...], first, last, tm)
    half = CONV_KERNEL // 2
    acc = jnp.zeros(cur_ref.shape, F32)
    for kk in range(CONV_KERNEL):
        off = HALO - half + kk
        acc = acc + w_ref[kk:kk + 1, :] * pad_ref[off:off + tm, :]
    a = acc + b_ref[...]
    mu = jnp.mean(a, axis=-1, keepdims=True)
    xc = a - mu
    var = jnp.mean(xc * xc, axis=-1, keepdims=True)
    y = xc * lax.rsqrt(var + EPS) * lg_ref[...] + lb_ref[...]
    o_ref[...] = (y * _sigmoid(y)).astype(o_ref.dtype)


def _short_conv_kernel(seq_info, prev_ref, cur_ref, next_ref, gb_ref, w_ref, o_ref, pad_ref):
    tm = cur_ref.shape[0]
    first, last = _seq_position(pl.program_id(0), *seq_info)
    _fill_padded(pad_ref, prev_ref[...], cur_ref[...], next_ref[...], first, last, tm)
    half = SHORT_KERNEL // 2
    acc = jnp.zeros(cur_ref.shape, F32)
    for kk in range(SHORT_KERNEL):
        off = HALO - half + kk
        acc = acc + w_ref[kk:kk + 1, :] * pad_ref[off:off + tm, :]
    o_ref[...] = (gb_ref[...] * acc).astype(o_ref.dtype)


def _halo_specs(n, tm, width):
    r = tm // HALO
    nh = n // HALO
    return [pl.BlockSpec((HALO, width), lambda i: (jnp.maximum(i * r - 1, 0), 0)),
            pl.BlockSpec((tm, width), lambda i: (i, 0)),
            pl.BlockSpec((HALO, width), lambda i: (jnp.minimum((i + 1) * r, nh - 1), 0))]


def _conv_module(a, w, b, lg, lb, n_ctx, ctx_seq, dec_seq):
    n, width = a.shape
    tm = TOKEN_BLOCK
    seq_info = (n_ctx // tm, ctx_seq // tm, dec_seq // tm)
    full = lambda i: (0, 0)
    return pl.pallas_call(
        functools.partial(_conv_module_kernel, seq_info),
        grid=(n // tm,),
        in_specs=_halo_specs(n, tm, width) + [pl.BlockSpec((CONV_KERNEL, width), full)]
        + [pl.BlockSpec((1, width), full)] * 3,
        out_specs=pl.BlockSpec((tm, width), lambda i: (i, 0)),
        out_shape=jax.ShapeDtypeStruct((n, width), BF16),
        scratch_shapes=[pltpu.VMEM((tm + 2 * HALO, width), F32)],
        compiler_params=_cparams("parallel"),
        name="conv_module",
    )(a, a, a, w, b, lg, lb)


def _short_conv(cv, gb, w, n_ctx, ctx_seq, dec_seq):
    n, width = cv.shape
    tm = TOKEN_BLOCK
    seq_info = (n_ctx // tm, ctx_seq // tm, dec_seq // tm)
    return pl.pallas_call(
        functools.partial(_short_conv_kernel, seq_info),
        grid=(n // tm,),
        in_specs=_halo_specs(n, tm, width) + [pl.BlockSpec((tm, width), lambda i: (i, 0)),
                                              pl.BlockSpec((SHORT_KERNEL, width), lambda i: (0, 0))],
        out_specs=pl.BlockSpec((tm, width), lambda i: (i, 0)),
        out_shape=jax.ShapeDtypeStruct((n, width), BF16),
        scratch_shapes=[pltpu.VMEM((tm + 2 * HALO, width), F32)],
        compiler_params=_cparams("parallel"),
        name="short_conv",
    )(cv, cv, cv, gb, w)


def _attn_heads(q, key_sets, o_ref):
    scale = HEAD_DIM ** -0.5
    grp = N_HEADS // N_KV_HEADS
    for h in range(N_HEADS):
        g = h // grp
        qh = q[:, h * HEAD_DIM:(h + 1) * HEAD_DIM]
        scores = [lax.dot_general(qh, k[:, g * HEAD_DIM:(g + 1) * HEAD_DIM], _NT,
                                  preferred_element_type=F32) * scale for k, _ in key_sets]
        m = scores[0].max(axis=-1, keepdims=True)
        for s in scores[1:]:
            m = jnp.maximum(m, s.max(axis=-1, keepdims=True))
        den = jnp.zeros_like(m)
        num = jnp.zeros((q.shape[0], HEAD_DIM), F32)
        for s, (_, v) in zip(scores, key_sets):
            p = jnp.exp(s - m)
            den = den + p.sum(axis=-1, keepdims=True)
            num = num + jnp.dot(p.astype(BF16), v[:, g * HEAD_DIM:(g + 1) * HEAD_DIM],
                                preferred_element_type=F32)
        o_ref[:, h * HEAD_DIM:(h + 1) * HEAD_DIM] = (num / den).astype(o_ref.dtype)


def _attn_ctx_kernel(q_ref, k_ref, v_ref, o_ref):
    _attn_heads(q_ref[...], [(k_ref[...].astype(BF16), v_ref[...].astype(BF16))], o_ref)


def _attn_lat_kernel(q_ref, ck_ref, cv_ref, k_ref, v_ref, o_ref):
    _attn_heads(q_ref[...], [(ck_ref[0, 0].astype(BF16), cv_ref[0, 0].astype(BF16)),
                             (k_ref[...].astype(BF16), v_ref[...].astype(BF16))], o_ref)


def _attention(q, k, v, cache_k, cache_v, layer_j, n_ctx, ctx_seq, dec_seq):
    n = q.shape[0]
    row = lambda b: (b, 0)
    ctx_out = pl.pallas_call(
        _attn_ctx_kernel,
        grid=(n_ctx // ctx_seq,),
        in_specs=[pl.BlockSpec((ctx_seq, ATTN_WIDTH), row),
                  pl.BlockSpec((ctx_seq, KV_WIDTH), row),
                  pl.BlockSpec((ctx_seq, KV_WIDTH), row)],
        out_specs=pl.BlockSpec((ctx_seq, ATTN_WIDTH), row),
        out_shape=jax.ShapeDtypeStruct((n_ctx, ATTN_WIDTH), BF16),
        compiler_params=_cparams("parallel"),
        name="attn_ctx",
    )(q, k, v)
    tq = TOKEN_BLOCK
    dec_batch = (n - n_ctx) // dec_seq
    past = cache_k.shape[2]
    qidx = lambda b, t: (n_ctx // tq + b * (dec_seq // tq) + t, 0)
    kidx = lambda b, t: (n_ctx // dec_seq + b, 0)
    cidx = lambda b, t: (b, layer_j, 0, 0)
    lat_out = pl.pallas_call(
        _attn_lat_kernel,
        grid=(dec_batch, dec_seq // tq),
        in_specs=[pl.BlockSpec((tq, ATTN_WIDTH), qidx),
                  pl.BlockSpec((1, 1, past, KV_WIDTH), cidx),
                  pl.BlockSpec((1, 1, past, KV_WIDTH), cidx),
                  pl.BlockSpec((dec_seq, KV_WIDTH), kidx),
                  pl.BlockSpec((dec_seq, KV_WIDTH), kidx)],
        out_specs=pl.BlockSpec((tq, ATTN_WIDTH), lambda b, t: (b * (dec_seq // tq) + t, 0)),
        out_shape=jax.ShapeDtypeStruct((n - n_ctx, ATTN_WIDTH), BF16),
        compiler_params=_cparams("parallel", "arbitrary"),
        name="attn_lat",
    )(q, cache_k, cache_v, k, v)
    return ctx_out, lat_out


def _out_proj_kernel(n_ctx_blocks, x_ref, mod_ref, yf_ref, yc_ref, yl_ref, wf_ref, ws_ref, o_ref):
    ys = jnp.where(pl.program_id(0) < n_ctx_blocks, yc_ref[...], yl_ref[...])
    y = (jnp.dot(yf_ref[...], wf_ref[...], preferred_element_type=F32)
         + jnp.dot(ys, ws_ref[...], preferred_element_type=F32))
    o_ref[...] = x_ref[...] + mod_ref[0, 2:3, :] * y


def _out_proj(x, mod, y_full, y_ctx, y_lat, w_bf, full_half, n_ctx, dec_seq):
    n, d = x.shape
    tm = TOKEN_BLOCK
    ncb, lb = n_ctx // tm, dec_seq // tm
    width = y_full.shape[1]
    row = lambda i: (i, 0)
    return pl.pallas_call(
        functools.partial(_out_proj_kernel, ncb),
        grid=(n // tm,),
        in_specs=[pl.BlockSpec((tm, d), row),
                  pl.BlockSpec((1, N_MOD, d), lambda i: (_group_of_block(i, ncb, lb), 0, 0)),
                  pl.BlockSpec((tm, width), row),
                  pl.BlockSpec((tm, width), lambda i: (jnp.minimum(i, ncb - 1), 0)),
                  pl.BlockSpec((tm, width), lambda i: (jnp.maximum(i - ncb, 0), 0)),
                  pl.BlockSpec((width, d), lambda i: (full_half, 0)),
                  pl.BlockSpec((width, d), lambda i: (1 - full_half, 0))],
        out_specs=pl.BlockSpec((tm, d), row),
        out_shape=jax.ShapeDtypeStruct((n, d), F32),
        compiler_params=_cparams("parallel"),
        name="out_proj",
    )(x, mod, y_full, y_ctx, y_lat, w_bf, w_bf)


def _odd_in_kernel(x_ref, mod_ref, g_ref, w_ref, f_ref, gb_ref, cv_ref):
    h = _modulate(x_ref[...], g_ref[...], mod_ref[0, 0:1, :], mod_ref[0, 1:2, :])
    proj = jnp.dot(h.astype(BF16), w_ref[...], preferred_element_type=F32)
    fw, sw = FOURIER_WIDTH, SHORT_WIDTH
    f_ref[...] = proj[:, :fw].astype(BF16)
    gb_ref[...] = proj[:, fw:fw + sw]
    cv_ref[...] = proj[:, fw + sw:fw + 2 * sw] * proj[:, fw + 2 * sw:]


def _odd_in(x, mod, g, w_bf, n_ctx, dec_seq):
    n, d = x.shape
    tm = TOKEN_BLOCK
    ncb, lb = n_ctx // tm, dec_seq // tm
    row = lambda i: (i, 0)
    full = lambda i: (0, 0)
    return pl.pallas_call(
        _odd_in_kernel,
        grid=(n // tm,),
        in_specs=[pl.BlockSpec((tm, d), row),
                  pl.BlockSpec((1, N_MOD, d), lambda i: (_group_of_block(i, ncb, lb), 0, 0)),
                  pl.BlockSpec((1, d), full),
                  pl.BlockSpec((d, w_bf.shape[1]), full)],
        out_specs=[pl.BlockSpec((tm, FOURIER_WIDTH), row),
                   pl.BlockSpec((tm, SHORT_WIDTH), row),
                   pl.BlockSpec((tm, SHORT_WIDTH), row)],
        out_shape=[jax.ShapeDtypeStruct((n, FOURIER_WIDTH), BF16),
                   jax.ShapeDtypeStruct((n, SHORT_WIDTH), F32),
                   jax.ShapeDtypeStruct((n, SHORT_WIDTH), F32)],
        compiler_params=_cparams("parallel"),
        name="odd_in",
    )(x, mod, g, w_bf)


def _dft_table_kernel(ac_ref, as_ref, bc_ref, bs_ref, c_ref, s_ref):
    ac = ac_ref[0]
    a_s = as_ref[0]
    bc = bc_ref[...]
    bs = bs_ref[...]
    c_ref[...] = (ac * bc - a_s * bs).astype(BF16)
    s_ref[...] = (a_s * bc + ac * bs).astype(BF16)


def _dft_tables(t):
    fine = 64
    coarse = t // fine
    k = np.arange(t, dtype=np.int64)
    ang_a = 2.0 * np.pi * ((np.arange(coarse, dtype=np.int64)[:, None] * fine * k[None, :]) % t) / t
    ang_b = 2.0 * np.pi * ((np.arange(fine, dtype=np.int64)[:, None] * k[None, :]) % t) / t
    ac = jnp.asarray(np.cos(ang_a), F32).reshape(coarse, 1, t)
    a_s = jnp.asarray(np.sin(ang_a), F32).reshape(coarse, 1, t)
    bc = jnp.asarray(np.cos(ang_b), F32)
    bs = jnp.asarray(np.sin(ang_b), F32)
    a_spec = pl.BlockSpec((1, 1, t), lambda i: (i, 0, 0))
    b_spec = pl.BlockSpec((fine, t), lambda i: (0, 0))
    o_spec = pl.BlockSpec((fine, t), lambda i: (i, 0))
    return pl.pallas_call(
        _dft_table_kernel,
        grid=(coarse,),
        in_specs=[a_spec, a_spec, b_spec, b_spec],
        out_specs=[o_spec, o_spec],
        out_shape=[jax.ShapeDtypeStruct((t, t), BF16)] * 2,
        compiler_params=_cparams("parallel"),
        name="dft_tables",
    )(ac, a_s, bc, bs)


def _small_dft(n):
    jk = (np.arange(n, dtype=np.int64)[:, None] * np.arange(n, dtype=np.int64)[None, :]) % n
    ang = 2.0 * np.pi * jk / n
    return np.cos(ang), np.sin(ang)


def _channel_dft(f, cb, sb):
    xc = jnp.dot(f, cb, preferred_element_type=F32).astype(BF16)
    xs = jnp.dot(f, sb, preferred_element_type=F32).astype(BF16)
    return xc, xs


def _fourier_ctx_kernel(scale, f_ref, cb_ref, sb_ref, ct_ref, st_ref, o_ref):
    xc, xs = _channel_dft(f_ref[...], cb_ref[...], sb_ref[...])
    y = (jnp.dot(ct_ref[...], xc, preferred_element_type=F32)
         - jnp.dot(st_ref[...], xs, preferred_element_type=F32))
    o_ref[...] = (y * scale).astype(o_ref.dtype)


def _fourier_lat_kernel(scale, f_ref, cb_ref, sb_ref, ct_ref, st_ref, o_ref, xc_ref, xs_ref):
    @pl.when(pl.program_id(1) == 0)
    def _():
        xc, xs = _channel_dft(f_ref[...], cb_ref[...], sb_ref[...])
        xc_ref[...] = xc
        xs_ref[...] = xs

    y = (jnp.dot(ct_ref[...], xc_ref[...], preferred_element_type=F32)
         - jnp.dot(st_ref[...], xs_ref[...], preferred_element_type=F32))
    o_ref[...] = (y * scale).astype(o_ref.dtype)


def _fourier(f, cb, sb, ct_ctx, st_ctx, ct_lat, st_lat, n_ctx, ctx_seq, dec_seq):
    n, width = f.shape
    full = lambda *_: (0, 0)
    ctx_out = pl.pallas_call(
        functools.partial(_fourier_ctx_kernel, 1.0 / math.sqrt(ctx_seq * FOURIER_GROUP_DIM)),
        grid=(n_ctx // ctx_seq,),
        in_specs=[pl.BlockSpec((ctx_seq, width), lambda b: (b, 0)),
                  pl.BlockSpec((width, width), full),
                  pl.BlockSpec((width, width), full),
                  pl.BlockSpec((ctx_seq, ctx_seq), full),
                  pl.BlockSpec((ctx_seq, ctx_seq), full)],
        out_specs=pl.BlockSpec((ctx_seq, width), lambda b: (b, 0)),
        out_shape=jax.ShapeDtypeStruct((n_ctx, width), BF16),
        compiler_params=_cparams("parallel"),
        name="fourier_ctx",
    )(f, cb, sb, ct_ctx, st_ctx)
    tr = TOKEN_BLOCK
    dec_batch = (n - n_ctx) // dec_seq
    fidx = lambda b, r: (n_ctx // dec_seq + b, 0)
    oidx = lambda b, r: (b * (dec_seq // tr) + r, 0)
    lat_out = pl.pallas_call(
        functools.partial(_fourier_lat_kernel, 1.0 / math.sqrt(dec_seq * FOURIER_GROUP_DIM)),
        grid=(dec_batch, dec_seq // tr),
        in_specs=[pl.BlockSpec((dec_seq, width), fidx),
                  pl.BlockSpec((width, width), full),
                  pl.BlockSpec((width, width), full),
                  pl.BlockSpec((tr, dec_seq), lambda b, r: (r, 0)),
                  pl.BlockSpec((tr, dec_seq), lambda b, r: (r, 0))],
        out_specs=pl.BlockSpec((tr, width), oidx),
        out_shape=jax.ShapeDtypeStruct((n - n_ctx, width), BF16),
        scratch_shapes=[pltpu.VMEM((dec_seq, width), BF16)] * 2,
        compiler_params=_cparams("parallel", "arbitrary"),
        name="fourier_lat",
    )(f, cb, sb, ct_lat, st_lat)
    return ctx_out, lat_out


K1 = PEER_TOPK + 1
KEY_VREGS = PEER_KEYS // SUBLANES
LIST_VREGS = -(-K1 // SUBLANES)


def _oddeven_merge_sort_pairs(n):
    pairs = []
    p = 1
    while p < n:
        k = p
        while k >= 1:
            for j in range(k % p, n - k, 2 * k):
                for i in range(min(k, n - j - k)):
                    if (i + j) // (2 * p) == (i + j + k) // (2 * p):
                        pairs.append((i + j, i + j + k))
            k //= 2
        p *= 2
    return pairs


_SORT_PAIRS = _oddeven_merge_sort_pairs(KEY_VREGS)


def _all_sublanes(x, op):
    for shift in (4, 2, 1):
        x = op(x, pltpu.roll(x, shift, 0))
    return x


def _row_iota():
    return lax.broadcasted_iota(jnp.int32, (SUBLANES, LANES), 0).astype(F32)


def _sorted_top(tiles):
    v = list(tiles)
    for i, j in _SORT_PAIRS:
        v[i], v[j] = jnp.maximum(v[i], v[j]), jnp.minimum(v[i], v[j])
    row8 = _row_iota()
    cnt = jnp.zeros((SUBLANES, LANES), F32)
    outs = [jnp.full((SUBLANES, LANES), NEG_INF, F32) for _ in range(LIST_VREGS)]
    for r in range(K1):
        depth = min(KEY_VREGS, K1 - r)
        m = _all_sublanes(v[0], jnp.maximum)
        eq = v[0] == m
        new = cnt + _all_sublanes(jnp.where(eq, 1.0, 0.0), jnp.add)
        for k in range(LIST_VREGS):
            if SUBLANES * k + SUBLANES - 1 >= r:
                rowk = row8 + float(SUBLANES * k)
                outs[k] = jnp.where(rowk >= cnt, jnp.where(rowk < new, m, outs[k]), outs[k])
        for d in range(depth - 1):
            v[d] = jnp.where(eq, v[d + 1], v[d])
        v[depth - 1] = jnp.where(eq, NEG_INF, v[depth - 1])
        cnt = new
    last = LIST_VREGS - 1
    outs[last] = jnp.where(row8 + float(SUBLANES * last) < float(K1), outs[last], NEG_INF)
    return outs


def _list_entry(lst, q):
    return jnp.broadcast_to(lst[q // SUBLANES][q % SUBLANES:q % SUBLANES + 1, :], (SUBLANES, LANES))


def _threshold_and_norm(a, b_vals):
    row8 = _row_iota()
    stack = [jnp.where(row8 < float(K1 // (q + 1)), a[0] + b_vals[q], NEG_INF) for q in range(K1)]
    rest = [a[k] + b_vals[0] for k in range(1, LIST_VREGS)]
    cmax = _list_entry(a, 0) + b_vals[0]
    cnt = jnp.zeros((SUBLANES, LANES), F32)
    hi = jnp.full((SUBLANES, LANES), NEG_INF, F32)
    lo = jnp.full((SUBLANES, LANES), NEG_INF, F32)
    z = jnp.zeros((SUBLANES, LANES), F32)
    for r in range(K1):
        depth = K1 - r
        top = stack[0]
        for t in rest:
            top = jnp.maximum(top, t)
        m = _all_sublanes(top, jnp.maximum)
        eq0 = stack[0] == m
        ones = jnp.where(eq0, 1.0, 0.0)
        for t in rest:
            ones = ones + jnp.where(t == m, 1.0, 0.0)
        c = _all_sublanes(ones, jnp.add)
        new = cnt + c
        z = z + jnp.where(cnt < float(PEER_TOPK), c * jnp.exp(m - cmax), 0.0)
        hi = jnp.where(cnt < float(PEER_TOPK), jnp.where(new >= float(PEER_TOPK), m, hi), hi)
        lo = jnp.where(cnt < float(K1), jnp.where(new >= float(K1), m, lo), lo)
        for d in range(depth - 1):
            stack[d] = jnp.where(eq0, stack[d + 1], stack[d])
        stack[depth - 1] = jnp.where(eq0, NEG_INF, stack[depth - 1])
        rest = [jnp.where(t == m, NEG_INF, t) for t in rest]
        cnt = new
    return 0.5 * (hi + lo), z


def _select_tile(s1, s2):
    a = _sorted_top(s1)
    b = _sorted_top(s2)
    b_vals = [_list_entry(b, q) for q in range(K1)]
    tmid, z = _threshold_and_norm(a, b_vals)
    a0 = _list_entry(a, 0)
    inv_z = 1.0 / z
    counts, ranks, e1, e2 = [], [], [], []
    for t1, t2 in zip(s1, s2):
        thr = tmid - t1
        cnt = jnp.zeros_like(t1)
        rank = jnp.zeros_like(t2)
        for q in range(PEER_TOPK):
            cnt = jnp.where(b_vals[q] >= thr, float(q + 1), cnt)
            rank = jnp.where(b_vals[q] > t2, float(q + 1), rank)
        counts.append(cnt)
        ranks.append(rank)
        e1.append(jnp.exp(t1 - a0) * inv_z)
        e2.append(jnp.exp(t2 - b_vals[0]))
    return counts, ranks, e1, e2


def _pack_pairs(tiles):
    words = [pltpu.bitcast(jnp.concatenate(tiles[c:c + 2], axis=0).astype(BF16), jnp.uint32)
             for c in range(0, len(tiles), 2)]
    return jnp.concatenate(words, axis=0)


def _peer_select_kernel(x_ref, mod_ref, g_ref, wq_ref, keys_ref, h_ref, cnt_ref, e1_ref, rank_ref, e2_ref, q_scr):
    h2 = _modulate(x_ref[...], g_ref[...], mod_ref[0, 3:4, :], mod_ref[0, 4:5, :]).astype(BF16)
    h_ref[...] = h2
    q = jnp.dot(h2, wq_ref[...], preferred_element_type=F32).astype(BF16)
    for c in range(2 * PEER_HEADS):
        q_scr[c] = q[:, c * PEER_HALF:(c + 1) * PEER_HALF]

    def head(h, carry):
        s1 = lax.dot_general(keys_ref[2 * h], q_scr[2 * h], _NT, preferred_element_type=F32)
        s2 = lax.dot_general(keys_ref[2 * h + 1], q_scr[2 * h + 1], _NT, preferred_element_type=F32)
        for lt in range(s1.shape[1] // LANES):
            lanes = slice(lt * LANES, (lt + 1) * LANES)
            tiles = lambda s: [s[SUBLANES * v:SUBLANES * (v + 1), lanes] for v in range(KEY_VREGS)]
            counts, ranks, e1, e2 = _select_tile(tiles(s1), tiles(s2))
            cnt_ref[h, :, lanes] = jnp.concatenate(counts, axis=0)
            e1_ref[h, :, lanes] = jnp.concatenate(e1, axis=0)
            rank_ref[h, :, lanes] = _pack_pairs(ranks)
            e2_ref[h, :, lanes] = _pack_pairs(e2)
        return carry

    lax.fori_loop(0, PEER_HEADS, head, 0)


def _peer_select(x, mod, g, wq_bf, keys_bf, n_ctx, dec_seq):
    n, d = x.shape
    tm = TOKEN_BLOCK
    ncb, lb = n_ctx // tm, dec_seq // tm
    row = lambda i: (i, 0)
    sel_spec = pl.BlockSpec((PEER_HEADS, PEER_KEYS, tm), lambda i: (0, 0, i))
    sel_f32 = jax.ShapeDtypeStruct((PEER_HEADS, PEER_KEYS, n), F32)
    packed = jax.ShapeDtypeStruct((PEER_HEADS, PEER_KEYS // 2, n), jnp.uint32)
    packed_spec = pl.BlockSpec((PEER_HEADS, PEER_KEYS // 2, tm), lambda i: (0, 0, i))
    return pl.pallas_call(
        _peer_select_kernel,
        grid=(n // tm,),
        in_specs=[pl.BlockSpec((tm, d), row),
                  pl.BlockSpec((1, N_MOD, d), lambda i: (_group_of_block(i, ncb, lb), 0, 0)),
                  pl.BlockSpec((1, d), lambda i: (0, 0)),
                  pl.BlockSpec(wq_bf.shape, lambda i: (0, 0)),
                  pl.BlockSpec(keys_bf.shape, lambda i: (0, 0, 0))],
        out_specs=[pl.BlockSpec((tm, d), row), sel_spec, sel_spec, packed_spec, packed_spec],
        out_shape=[jax.ShapeDtypeStruct((n, d), BF16), sel_f32, sel_f32, packed, packed],
        scratch_shapes=[pltpu.VMEM((2 * PEER_HEADS, tm, PEER_HALF), BF16)],
        compiler_params=_cparams("parallel"),
        name="peer_select",
    )(x, mod, g, wq_bf, keys_bf)


def _gelu_tanh(x):
    c = math.sqrt(2.0 / math.pi)
    return x * (0.5 * (1.0 + jnp.tanh(c * (x + 0.044715 * (x * x * x)))))


PACKED_ROWS = 2 * SUBLANES


def _peer_dense_kernel(x_ref, mod_ref, h_ref, u_ref, vt_ref, cnt_ref, e1_ref, rank_ref, e2_ref, o_ref,
                       acc_ref, act_ref, gt_ref):
    eb = pl.program_id(1)

    @pl.when(eb == 0)
    def _():
        acc_ref[...] = jnp.zeros_like(acc_ref)

    h2 = h_ref[...]
    tb = h2.shape[0]
    zero = jnp.zeros((), BF16)
    act_ref[...] = lax.dot_general(u_ref[...], h2, _NT, preferred_element_type=F32)

    for il in range(DENSE_EXPERTS // PEER_KEYS):
        for lt in range(tb // LANES):
            lanes = slice(lt * LANES, (lt + 1) * LANES)
            bcast = lambda ref, h: jnp.broadcast_to(ref[h, il:il + 1, lanes], (PACKED_ROWS, LANES)).astype(BF16)
            n_chunks = PEER_KEYS // PACKED_ROWS
            w = [None] * n_chunks
            for h in range(PEER_HEADS):
                cnt, e1 = bcast(cnt_ref, h), bcast(e1_ref, h)
                for rc in range(n_chunks):
                    jr = slice(rc * SUBLANES, (rc + 1) * SUBLANES)
                    rank = pltpu.bitcast(rank_ref[h, jr, lanes], BF16)
                    term = jnp.where(rank < cnt, pltpu.bitcast(e2_ref[h, jr, lanes], BF16), zero) * e1
                    w[rc] = term if w[rc] is None else w[rc] + term
            for rc in range(n_chunks):
                er = slice(il * PEER_KEYS + rc * PACKED_ROWS, il * PEER_KEYS + (rc + 1) * PACKED_ROWS)
                gt_ref[er, lanes] = w[rc] * _gelu_tanh(act_ref[er, lanes].astype(BF16))

    acc_ref[...] += jnp.dot(vt_ref[...], gt_ref[...], preferred_element_type=F32)

    @pl.when(eb == pl.num_programs(1) - 1)
    def _():
        o_ref[...] = x_ref[...] + mod_ref[0, 5:6, :] * acc_ref[...].T


def _peer_dense(x, mod, h2, u_bf, vt_bf, cnt, e1, rank, e2, n_ctx, dec_seq):
    n, d = x.shape
    tb, eb = DENSE_TOKENS, DENSE_EXPERTS
    n_exp = u_bf.shape[0]
    ncb, lb = n_ctx // tb, dec_seq // tb
    row = lambda t, e: (t, 0)
    key_spec = pl.BlockSpec((PEER_HEADS, eb // PEER_KEYS, tb), lambda t, e: (0, e, t))
    packed_spec = pl.BlockSpec((PEER_HEADS, PEER_KEYS // 2, tb), lambda t, e: (0, 0, t))
    return pl.pallas_call(
        _peer_dense_kernel,
        grid=(n // tb, n_exp // eb),
        in_specs=[pl.BlockSpec((tb, d), row),
                  pl.BlockSpec((1, N_MOD, d), lambda t, e: (_group_of_block(t, ncb, lb), 0, 0)),
                  pl.BlockSpec((tb, d), row),
                  pl.BlockSpec((eb, d), lambda t, e: (e, 0)),
                  pl.BlockSpec((d, eb), lambda t, e: (0, e)),
                  key_spec, key_spec, packed_spec, packed_spec],
        out_specs=pl.BlockSpec((tb, d), row),
        out_shape=jax.ShapeDtypeStruct((n, d), F32),
        scratch_shapes=[pltpu.VMEM((d, tb), F32),
                        pltpu.VMEM((eb, tb), F32),
                        pltpu.VMEM((eb, tb), BF16)],
        compiler_params=_cparams("parallel", "arbitrary"),
        name="peer_dense",
    )(x, mod, h2, u_bf, vt_bf, cnt, e1, rank, e2)


def _final_norm_kernel(x_ref, g_ref, o_ref):
    x = x_ref[...]
    ms = jnp.mean(x * x, axis=-1, keepdims=True)
    o_ref[...] = x * lax.rsqrt(ms + EPS) * g_ref[...]


def _final_norm(x, g, first_row, n_rows):
    d = x.shape[1]
    tm = TOKEN_BLOCK
    first_block = first_row // tm
    return pl.pallas_call(
        _final_norm_kernel,
        grid=(n_rows // tm,),
        in_specs=[pl.BlockSpec((tm, d), lambda i: (first_block + i, 0)), pl.BlockSpec((1, d), lambda i: (0, 0))],
        out_specs=pl.BlockSpec((tm, d), lambda i: (i, 0)),
        out_shape=jax.ShapeDtypeStruct((n_rows, d), F32),
        compiler_params=_cparams("parallel"),
        name="final_norm",
    )(x, g)


def _rope_tables(n_tokens):
    rows = n_tokens // GRID_W
    row = np.repeat(np.arange(rows), GRID_W).astype(np.float32)
    col = np.tile(np.arange(GRID_W), rows).astype(np.float32)
    half = HEAD_DIM // 2
    inv = (np.float32(ROPE_THETA) ** (-np.arange(0, half, 2, dtype=np.float32) / np.float32(half))).astype(np.float32)
    ang = np.concatenate([row[:, None] * inv, col[:, None] * inv], axis=-1).astype(np.float32)
    cos = np.repeat(np.cos(ang.astype(np.float64)), 2, axis=-1)
    sin = np.repeat(np.sin(ang.astype(np.float64)), 2, axis=-1)
    sin = sin * np.tile(np.array([-1.0, 1.0]), HEAD_DIM // 2)
    return (jnp.asarray(np.tile(cos, (1, N_HEADS)), F32), jnp.asarray(np.tile(sin, (1, N_HEADS)), F32))


def _block_diag(block, count):
    n = block.shape[0]
    out = np.zeros((n * count, n * count), np.float64)
    for c in range(count):
        out[c * n:(c + 1) * n, c * n:(c + 1) * n] = block
    return out


def kernel(x_prompt, x_sample, cache_k, cache_v, c, c_ctx, w_mod, b_mod, norm_mix_g, norm_ffn_g, w_in_even, w_out_even, conv_dw_w, conv_dw_b, conv_ln_g, conv_ln_b, q_norm_g, k_norm_g, w_in_odd, w_out_odd, short_conv_w, peer_wq, peer_keys, peer_u, peer_v, final_norm_g):
    batch, ctx_seq, d = x_prompt.shape
    dec_batch, dec_seq, _ = x_sample.shape
    depth = w_mod.shape[0]
    n_ctx = batch * ctx_seq
    past = cache_k.shape[2]

    x = jnp.concatenate([x_prompt.reshape(n_ctx, d), x_sample.reshape(dec_batch * dec_seq, d)], axis=0)
    mods = _modulation(jnp.concatenate([c_ctx[None, :], c], axis=0), w_mod, b_mod)

    cos, sin = _rope_tables(dec_seq)
    bd = jnp.asarray(_block_diag(np.ones((HEAD_DIM, HEAD_DIM)), N_HEADS), BF16)
    c128, s128 = _small_dft(FOURIER_GROUP_DIM)
    cb = jnp.asarray(_block_diag(c128, FOURIER_GROUPS), BF16)
    sb = jnp.asarray(_block_diag(s128, FOURIER_GROUPS), BF16)
    c_ctx_t, s_ctx_t = _small_dft(ctx_seq)
    ct_ctx, st_ctx = jnp.asarray(c_ctx_t, BF16), jnp.asarray(s_ctx_t, BF16)
    if depth > 1:
        ct_lat, st_lat = _dft_tables(dec_seq)
    ck = cache_k.reshape(dec_batch, -1, past, KV_WIDTH)
    cv = cache_v.reshape(dec_batch, -1, past, KV_WIDTH)

    new_k, new_v = [], []
    for l in range(depth):
        mod = mods[l]
        j = l // 2
        g_mix = norm_mix_g[l][None, :]
        if l % 2 == 0:
            qg = jnp.tile(q_norm_g[j], N_HEADS)[None, :]
            kg = jnp.tile(k_norm_g[j], N_KV_HEADS)[None, :]
            a, q, k, v = _even_in(x, mod, g_mix, w_in_even[j].astype(BF16), qg, kg, bd, cos, sin, n_ctx, dec_seq)
            new_k.append(k[:n_ctx].reshape(batch, ctx_seq, N_KV_HEADS, HEAD_DIM))
            new_v.append(v[:n_ctx].reshape(batch, ctx_seq, N_KV_HEADS, HEAD_DIM))
            y1 = _conv_module(a, conv_dw_w[j], conv_dw_b[j][None, :], conv_ln_g[j][None, :],
                              conv_ln_b[j][None, :], n_ctx, ctx_seq, dec_seq)
            y_ctx, y_lat = _attention(q, k, v, ck, cv, j, n_ctx, ctx_seq, dec_seq)
            x = _out_proj(x, mod, y1, y_ctx, y_lat, w_out_even[j].astype(BF16), 0, n_ctx, dec_seq)
        else:
            f, gb, cvv = _odd_in(x, mod, g_mix, w_in_odd[j].astype(BF16), n_ctx, dec_seq)
            y_ctx, y_lat = _fourier(f, cb, sb, ct_ctx, st_ctx, ct_lat, st_lat, n_ctx, ctx_seq, dec_seq)
            y2 = _short_conv(cvv, gb, short_conv_w[j], n_ctx, ctx_seq, dec_seq)
            x = _out_proj(x, mod, y2, y_ctx, y_lat, w_out_odd[j].astype(BF16), 1, n_ctx, dec_seq)
        keys_bf = peer_keys[l].reshape(2 * PEER_HEADS, PEER_KEYS, PEER_HALF).astype(BF16)
        h2, cnt, e1, rank, e2 = _peer_select(x, mod, norm_ffn_g[l][None, :], peer_wq[l].astype(BF16), keys_bf,
                                             n_ctx, dec_seq)
        x = _peer_dense(x, mod, h2, peer_u[l].astype(BF16), peer_v[l].T.astype(BF16), cnt, e1, rank, e2,
                        n_ctx, dec_seq)

    y_prompt = _final_norm(x, final_norm_g[None, :], 0, n_ctx).reshape(batch, ctx_seq, d)
    y_sample = _final_norm(x, final_norm_g[None, :], n_ctx, dec_batch * dec_seq).reshape(dec_batch, dec_seq, d)
    return (y_prompt, y_sample, jnp.stack(new_k, axis=1), jnp.stack(new_v, axis=1))
```

```python
import functools
import math

import numpy as np
import jax
import jax.numpy as jnp
from jax import lax
from jax.experimental import pallas as pl
from jax.experimental.pallas import tpu as pltpu

F32 = jnp.float32
BF16 = jnp.bfloat16
EPS = 1e-6
NEG_INF = float("-inf")

GRID_W = 64
ROPE_THETA = 10000.0
N_MOD = 6
CONV_WIDTH = 512
CONV_KERNEL = 31
N_HEADS = 8
N_KV_HEADS = 2
HEAD_DIM = 64
ATTN_WIDTH = N_HEADS * HEAD_DIM
KV_WIDTH = N_KV_HEADS * HEAD_DIM
FOURIER_GROUPS = 4
FOURIER_GROUP_DIM = 128
FOURIER_WIDTH = FOURIER_GROUPS * FOURIER_GROUP_DIM
SHORT_WIDTH = 512
SHORT_KERNEL = 3
PEER_HEADS = 8
PEER_KEYS = 128
PEER_HALF = 128
PEER_TOPK = 16

LANES = 128
SUBLANES = 8
HALO = 16
TOKEN_BLOCK = 256
CONV_ROWS = 64
SELECT_TOKENS = 512
DENSE_TOKENS = 512
DENSE_EXPERTS = 1024
VMEM_LIMIT = 56 * 1024 * 1024

_NT = (((1,), (1,)), ((), ()))


def _cparams(*sem, vmem=VMEM_LIMIT):
    return pltpu.CompilerParams(dimension_semantics=sem, vmem_limit_bytes=vmem)


def _sigmoid(x):
    return 1.0 / (1.0 + jnp.exp(-x))


def _modulate(x, g, shift, scale):
    ms = jnp.mean(x * x, axis=-1, keepdims=True)
    y = x * lax.rsqrt(ms + EPS) * g
    return y * (1.0 + scale) + shift


def _group_of_block(i, n_ctx_blocks, blocks_per_latent):
    return jnp.where(i < n_ctx_blocks, 0, 1 + (i - n_ctx_blocks) // blocks_per_latent)


def _mod_kernel(cond_ref, w_ref, b_ref, o_ref):
    c = cond_ref[...]
    s = c * _sigmoid(c)
    o_ref[0] = jnp.dot(s, w_ref[0], preferred_element_type=F32) + b_ref[0]


def _modulation(cond, w_mod, b_mod):
    n_layers, d, f = w_mod.shape
    g = cond.shape[0]
    gp = -(-g // SUBLANES) * SUBLANES
    cond_p = jnp.zeros((gp, d), F32).at[:g].set(cond)
    cb = 1536
    out = pl.pallas_call(
        _mod_kernel,
        grid=(n_layers, f // cb),
        in_specs=[pl.BlockSpec((gp, d), lambda l, j: (0, 0)),
                  pl.BlockSpec((1, d, cb), lambda l, j: (l, 0, j)),
                  pl.BlockSpec((1, 1, cb), lambda l, j: (l, 0, j))],
        out_specs=pl.BlockSpec((1, gp, cb), lambda l, j: (l, 0, j)),
        out_shape=jax.ShapeDtypeStruct((n_layers, gp, f), F32),
        compiler_params=_cparams("parallel", "arbitrary"),
        name="modulation",
    )(cond_p, w_mod, b_mod.reshape(n_layers, 1, f))
    return out[:, :g].reshape(n_layers, g, N_MOD, d)


def _head_norm(t, bd, gain):
    tt = t * t
    hi = tt.astype(BF16)
    lo = (tt - hi.astype(F32)).astype(BF16)
    ss = (jnp.dot(hi, bd, preferred_element_type=F32) + jnp.dot(lo, bd, preferred_element_type=F32))
    return t * lax.rsqrt(ss * (1.0 / HEAD_DIM) + EPS) * gain


def _pair_swap(x):
    n = x.shape[-1]
    lane = lax.broadcasted_iota(jnp.int32, x.shape, 1)
    return jnp.where((lane & 1) == 0, pltpu.roll(x, n - 1, 1), pltpu.roll(x, 1, 1))


def _even_in_kernel(n_ctx_blocks, x_ref, mod_ref, g_ref, w_ref, qg_ref, kg_ref, bd_ref, cos_ref, sin_ref,
                    a_ref, q_ref, k_ref, v_ref):
    i = pl.program_id(0)
    h = _modulate(x_ref[...], g_ref[...], mod_ref[0, 0:1, :], mod_ref[0, 1:2, :])
    proj = jnp.dot(h.astype(BF16), w_ref[...], preferred_element_type=F32)
    s1, s2, s3, s4 = CONV_WIDTH, 2 * CONV_WIDTH, 2 * CONV_WIDTH + ATTN_WIDTH, 2 * CONV_WIDTH + ATTN_WIDTH + KV_WIDTH
    a_ref[...] = proj[:, :s1] * _sigmoid(proj[:, s1:s2])
    bd = bd_ref[...]
    qn = _head_norm(proj[:, s2:s3], bd, qg_ref[...])
    kn = _head_norm(proj[:, s3:s4], bd[:KV_WIDTH, :KV_WIDTH], kg_ref[...])
    v_ref[...] = proj[:, s4:]

    @pl.when(i < n_ctx_blocks)
    def _():
        q_ref[...] = qn.astype(BF16)
        k_ref[...] = kn

    @pl.when(i >= n_ctx_blocks)
    def _():
        cos = cos_ref[...]
        sin = sin_ref[...]
        q_ref[...] = (qn * cos + _pair_swap(qn) * sin).astype(BF16)
        k_ref[...] = kn * cos[:, :KV_WIDTH] + _pair_swap(kn) * sin[:, :KV_WIDTH]


def _even_in(x, mod, g, w_bf, qg, kg, bd, cos, sin, n_ctx, dec_seq):
    n, d = x.shape
    tm = TOKEN_BLOCK
    ncb, lb = n_ctx // tm, dec_seq // tm
    grp = lambda i: (_group_of_block(i, ncb, lb), 0, 0)
    rope_idx = lambda i: (jnp.maximum(i - ncb, 0) % lb, 0)
    full = lambda i: (0, 0)
    row = lambda i: (i, 0)
    f_in = w_bf.shape[1]
    return pl.pallas_call(
        functools.partial(_even_in_kernel, ncb),
        grid=(n // tm,),
        in_specs=[pl.BlockSpec((tm, d), row),
                  pl.BlockSpec((1, N_MOD, d), grp),
                  pl.BlockSpec((1, d), full),
                  pl.BlockSpec((d, f_in), full),
                  pl.BlockSpec((1, ATTN_WIDTH), full),
                  pl.BlockSpec((1, KV_WIDTH), full),
                  pl.BlockSpec((ATTN_WIDTH, ATTN_WIDTH), full),
                  pl.BlockSpec((tm, ATTN_WIDTH), rope_idx),
                  pl.BlockSpec((tm, ATTN_WIDTH), rope_idx)],
        out_specs=[pl.BlockSpec((tm, CONV_WIDTH), row),
                   pl.BlockSpec((tm, ATTN_WIDTH), row),
                   pl.BlockSpec((tm, KV_WIDTH), row),
                   pl.BlockSpec((tm, KV_WIDTH), row)],
        out_shape=[jax.ShapeDtypeStruct((n, CONV_WIDTH), F32),
                   jax.ShapeDtypeStruct((n, ATTN_WIDTH), BF16),
                   jax.ShapeDtypeStruct((n, KV_WIDTH), F32),
                   jax.ShapeDtypeStruct((n, KV_WIDTH), F32)],
        compiler_params=_cparams("parallel"),
        name="even_in",
    )(x, mod, g, w_bf, qg, kg, bd, cos, sin)


def _seq_position(i, n_ctx_blocks, ctx_seq_blocks, lat_seq_blocks):
    is_ctx = i < n_ctx_blocks
    sb = jnp.where(is_ctx, ctx_seq_blocks, lat_seq_blocks)
    pos = jnp.where(is_ctx, i, i - n_ctx_blocks) % sb
    return pos == 0, pos == sb - 1


def _fill_padded(pad_ref, prev, cur, nxt, first, last, tm):
    pad_ref[0:HALO, :] = jnp.where(first, 0.0, prev)
    pad_ref[HALO:HALO + tm, :] = cur
    pad_ref[HALO + tm:2 * HALO + tm, :] = jnp.where(last, 0.0, nxt)


def _conv_module_kernel(seq_info, prev_ref, cur_ref, next_ref, w_ref, b_ref, lg_ref, lb_ref, o_ref, pad_ref, shift_ref):
    tm = cur_ref.shape[0]
    first, last = _seq_position(pl.program_id(0), *seq_info)
    _fill_padded(pad_ref, prev_ref[...], cur_ref[...], next_ref[...], first, last, tm)
    first_off = HALO - CONV_KERNEL // 2
    span = -(-(first_off + CONV_KERNEL - 1) // SUBLANES)
    width = cur_ref.shape[1]
    for r in range(SUBLANES):
        shift_ref[r] = pad_ref[r:r + shift_ref.shape[1], :]
    for c0 in range(0, tm, CONV_ROWS):
        pieces = []
        for lt in range(width // LANES):
            lanes = slice(lt * LANES, (lt + 1) * LANES)
            acc = jnp.zeros((CONV_ROWS, LANES), F32)
            for r in range(SUBLANES):
                for m in range(span):
                    kk = SUBLANES * m + r - first_off
                    if 0 <= kk < CONV_KERNEL:
                        rows = slice(c0 + SUBLANES * m, c0 + SUBLANES * m + CONV_ROWS)
                        acc = acc + w_ref[kk:kk + 1, lanes] * shift_ref[r, rows, lanes]
            pieces.append(acc)
        a = jnp.concatenate(pieces, axis=1) + b_ref[...]
        mu = jnp.mean(a, axis=-1, keepdims=True)
        xc = a - mu
        var = jnp.mean(xc * xc, axis=-1, keepdims=True)
        y = xc * lax.rsqrt(var + EPS) * lg_ref[...] + lb_ref[...]
        o_ref[c0:c0 + CONV_ROWS, :] = (y * _sigmoid(y)).astype(o_ref.dtype)


def _short_conv_kernel(seq_info, prev_ref, cur_ref, next_ref, gb_ref, w_ref, o_ref, pad_ref):
    tm = cur_ref.shape[0]
    first, last = _seq_position(pl.program_id(0), *seq_info)
    _fill_padded(pad_ref, prev_ref[...], cur_ref[...], next_ref[...], first, last, tm)
    half = SHORT_KERNEL // 2
    acc = jnp.zeros(cur_ref.shape, F32)
    for kk in range(SHORT_KERNEL):
        off = HALO - half + kk
        acc = acc + w_ref[kk:kk + 1, :] * pad_ref[off:off + tm, :]
    o_ref[...] = (gb_ref[...] * acc).astype(o_ref.dtype)


def _halo_specs(n, tm, width):
    r = tm // HALO
    nh = n // HALO
    return [pl.BlockSpec((HALO, width), lambda i: (jnp.maximum(i * r - 1, 0), 0)),
            pl.BlockSpec((tm, width), lambda i: (i, 0)),
            pl.BlockSpec((HALO, width), lambda i: (jnp.minimum((i + 1) * r, nh - 1), 0))]


def _conv_module(a, w, b, lg, lb, n_ctx, ctx_seq, dec_seq):
    n, width = a.shape
    tm = TOKEN_BLOCK
    seq_info = (n_ctx // tm, ctx_seq // tm, dec_seq // tm)
    full = lambda i: (0, 0)
    return pl.pallas_call(
        functools.partial(_conv_module_kernel, seq_info),
        grid=(n // tm,),
        in_specs=_halo_specs(n, tm, width) + [pl.BlockSpec((CONV_KERNEL, width), full)]
        + [pl.BlockSpec((1, width), full)] * 3,
        out_specs=pl.BlockSpec((tm, width), lambda i: (i, 0)),
        out_shape=jax.ShapeDtypeStruct((n, width), BF16),
        scratch_shapes=[pltpu.VMEM((tm + 2 * HALO, width), F32),
                        pltpu.VMEM((SUBLANES, tm + 2 * HALO - SUBLANES, width), F32)],
        compiler_params=_cparams("parallel"),
        name="conv_module",
    )(a, a, a, w, b, lg, lb)


def _short_conv(cv, gb, w, n_ctx, ctx_seq, dec_seq):
    n, width = cv.shape
    tm = TOKEN_BLOCK
    seq_info = (n_ctx // tm, ctx_seq // tm, dec_seq // tm)
    return pl.pallas_call(
        functools.partial(_short_conv_kernel, seq_info),
        grid=(n // tm,),
        in_specs=_halo_specs(n, tm, width) + [pl.BlockSpec((tm, width), lambda i: (i, 0)),
                                              pl.BlockSpec((SHORT_KERNEL, width), lambda i: (0, 0))],
        out_specs=pl.BlockSpec((tm, width), lambda i: (i, 0)),
        out_shape=jax.ShapeDtypeStruct((n, width), BF16),
        scratch_shapes=[pltpu.VMEM((tm + 2 * HALO, width), F32)],
        compiler_params=_cparams("parallel"),
        name="short_conv",
    )(cv, cv, cv, gb, w)


def _attn_heads(q, key_sets, o_ref):
    scale = HEAD_DIM ** -0.5
    grp = N_HEADS // N_KV_HEADS
    for h in range(N_HEADS):
        g = h // grp
        qh = q[:, h * HEAD_DIM:(h + 1) * HEAD_DIM]
        scores = [lax.dot_general(qh, k[:, g * HEAD_DIM:(g + 1) * HEAD_DIM], _NT,
                                  preferred_element_type=F32) * scale for k, _ in key_sets]
        m = scores[0].max(axis=-1, keepdims=True)
        for s in scores[1:]:
            m = jnp.maximum(m, s.max(axis=-1, keepdims=True))
        den = jnp.zeros_like(m)
        num = jnp.zeros((q.shape[0], HEAD_DIM), F32)
        for s, (_, v) in zip(scores, key_sets):
            p = jnp.exp(s - m)
            den = den + p.sum(axis=-1, keepdims=True)
            num = num + jnp.dot(p.astype(BF16), v[:, g * HEAD_DIM:(g + 1) * HEAD_DIM],
                                preferred_element_type=F32)
        o_ref[:, h * HEAD_DIM:(h + 1) * HEAD_DIM] = (num / den).astype(o_ref.dtype)


def _attn_ctx_kernel(q_ref, k_ref, v_ref, o_ref):
    _attn_heads(q_ref[...], [(k_ref[...].astype(BF16), v_ref[...].astype(BF16))], o_ref)


def _attn_lat_kernel(q_ref, ck_ref, cv_ref, k_ref, v_ref, o_ref):
    _attn_heads(q_ref[...], [(ck_ref[0, 0].astype(BF16), cv_ref[0, 0].astype(BF16)),
                             (k_ref[...].astype(BF16), v_ref[...].astype(BF16))], o_ref)


def _attention(q, k, v, cache_k, cache_v, layer_j, n_ctx, ctx_seq, dec_seq):
    n = q.shape[0]
    row = lambda b: (b, 0)
    ctx_out = pl.pallas_call(
        _attn_ctx_kernel,
        grid=(n_ctx // ctx_seq,),
        in_specs=[pl.BlockSpec((ctx_seq, ATTN_WIDTH), row),
                  pl.BlockSpec((ctx_seq, KV_WIDTH), row),
                  pl.BlockSpec((ctx_seq, KV_WIDTH), row)],
        out_specs=pl.BlockSpec((ctx_seq, ATTN_WIDTH), row),
        out_shape=jax.ShapeDtypeStruct((n_ctx, ATTN_WIDTH), BF16),
        compiler_params=_cparams("parallel"),
        name="attn_ctx",
    )(q, k, v)
    tq = TOKEN_BLOCK
    dec_batch = (n - n_ctx) // dec_seq
    past = cache_k.shape[2]
    qidx = lambda b, t: (n_ctx // tq + b * (dec_seq // tq) + t, 0)
    kidx = lambda b, t: (n_ctx // dec_seq + b, 0)
    cidx = lambda b, t: (b, layer_j, 0, 0)
    lat_out = pl.pallas_call(
        _attn_lat_kernel,
        grid=(dec_batch, dec_seq // tq),
        in_specs=[pl.BlockSpec((tq, ATTN_WIDTH), qidx),
                  pl.BlockSpec((1, 1, past, KV_WIDTH), cidx),
                  pl.BlockSpec((1, 1, past, KV_WIDTH), cidx),
                  pl.BlockSpec((dec_seq, KV_WIDTH), kidx),
                  pl.BlockSpec((dec_seq, KV_WIDTH), kidx)],
        out_specs=pl.BlockSpec((tq, ATTN_WIDTH), lambda b, t: (b * (dec_seq // tq) + t, 0)),
        out_shape=jax.ShapeDtypeStruct((n - n_ctx, ATTN_WIDTH), BF16),
        compiler_params=_cparams("parallel", "arbitrary"),
        name="attn_lat",
    )(q, cache_k, cache_v, k, v)
    return ctx_out, lat_out


def _out_proj_kernel(n_ctx_blocks, x_ref, mod_ref, yf_ref, yc_ref, yl_ref, wf_ref, ws_ref, o_ref):
    ys = jnp.where(pl.program_id(0) < n_ctx_blocks, yc_ref[...], yl_ref[...])
    y = (jnp.dot(yf_ref[...], wf_ref[...], preferred_element_type=F32)
         + jnp.dot(ys, ws_ref[...], preferred_element_type=F32))
    o_ref[...] = x_ref[...] + mod_ref[0, 2:3, :] * y


def _out_proj(x, mod, y_full, y_ctx, y_lat, w_bf, full_half, n_ctx, dec_seq):
    n, d = x.shape
    tm = TOKEN_BLOCK
    ncb, lb = n_ctx // tm, dec_seq // tm
    width = y_full.shape[1]
    row = lambda i: (i, 0)
    return pl.pallas_call(
        functools.partial(_out_proj_kernel, ncb),
        grid=(n // tm,),
        in_specs=[pl.BlockSpec((tm, d), row),
                  pl.BlockSpec((1, N_MOD, d), lambda i: (_group_of_block(i, ncb, lb), 0, 0)),
                  pl.BlockSpec((tm, width), row),
                  pl.BlockSpec((tm, width), lambda i: (jnp.minimum(i, ncb - 1), 0)),
                  pl.BlockSpec((tm, width), lambda i: (jnp.maximum(i - ncb, 0), 0)),
                  pl.BlockSpec((width, d), lambda i: (full_half, 0)),
                  pl.BlockSpec((width, d), lambda i: (1 - full_half, 0))],
        out_specs=pl.BlockSpec((tm, d), row),
        out_shape=jax.ShapeDtypeStruct((n, d), F32),
        compiler_params=_cparams("parallel"),
        name="out_proj",
    )(x, mod, y_full, y_ctx, y_lat, w_bf, w_bf)


def _odd_in_kernel(x_ref, mod_ref, g_ref, w_ref, f_ref, gb_ref, cv_ref):
    h = _modulate(x_ref[...], g_ref[...], mod_ref[0, 0:1, :], mod_ref[0, 1:2, :])
    proj = jnp.dot(h.astype(BF16), w_ref[...], preferred_element_type=F32)
    fw, sw = FOURIER_WIDTH, SHORT_WIDTH
    f_ref[...] = proj[:, :fw].astype(BF16)
    gb_ref[...] = proj[:, fw:fw + sw]
    cv_ref[...] = proj[:, fw + sw:fw + 2 * sw] * proj[:, fw + 2 * sw:]


def _odd_in(x, mod, g, w_bf, n_ctx, dec_seq):
    n, d = x.shape
    tm = TOKEN_BLOCK
    ncb, lb = n_ctx // tm, dec_seq // tm
    row = lambda i: (i, 0)
    full = lambda i: (0, 0)
    return pl.pallas_call(
        _odd_in_kernel,
        grid=(n // tm,),
        in_specs=[pl.BlockSpec((tm, d), row),
                  pl.BlockSpec((1, N_MOD, d), lambda i: (_group_of_block(i, ncb, lb), 0, 0)),
                  pl.BlockSpec((1, d), full),
                  pl.BlockSpec((d, w_bf.shape[1]), full)],
        out_specs=[pl.BlockSpec((tm, FOURIER_WIDTH), row),
                   pl.BlockSpec((tm, SHORT_WIDTH), row),
                   pl.BlockSpec((tm, SHORT_WIDTH), row)],
        out_shape=[jax.ShapeDtypeStruct((n, FOURIER_WIDTH), BF16),
                   jax.ShapeDtypeStruct((n, SHORT_WIDTH), F32),
                   jax.ShapeDtypeStruct((n, SHORT_WIDTH), F32)],
        compiler_params=_cparams("parallel"),
        name="odd_in",
    )(x, mod, g, w_bf)


def _dft_table_kernel(ac_ref, as_ref, bc_ref, bs_ref, c_ref, s_ref):
    ac = ac_ref[0]
    a_s = as_ref[0]
    bc = bc_ref[...]
    bs = bs_ref[...]
    c_ref[...] = (ac * bc - a_s * bs).astype(BF16)
    s_ref[...] = (a_s * bc + ac * bs).astype(BF16)


def _dft_tables(t):
    fine = 64
    coarse = t // fine
    k = np.arange(t, dtype=np.int64)
    ang_a = 2.0 * np.pi * ((np.arange(coarse, dtype=np.int64)[:, None] * fine * k[None, :]) % t) / t
    ang_b = 2.0 * np.pi * ((np.arange(fine, dtype=np.int64)[:, None] * k[None, :]) % t) / t
    ac = jnp.asarray(np.cos(ang_a), F32).reshape(coarse, 1, t)
    a_s = jnp.asarray(np.sin(ang_a), F32).reshape(coarse, 1, t)
    bc = jnp.asarray(np.cos(ang_b), F32)
    bs = jnp.asarray(np.sin(ang_b), F32)
    a_spec = pl.BlockSpec((1, 1, t), lambda i: (i, 0, 0))
    b_spec = pl.BlockSpec((fine, t), lambda i: (0, 0))
    o_spec = pl.BlockSpec((fine, t), lambda i: (i, 0))
    return pl.pallas_call(
        _dft_table_kernel,
        grid=(coarse,),
        in_specs=[a_spec, a_spec, b_spec, b_spec],
        out_specs=[o_spec, o_spec],
        out_shape=[jax.ShapeDtypeStruct((t, t), BF16)] * 2,
        compiler_params=_cparams("parallel"),
        name="dft_tables",
    )(ac, a_s, bc, bs)


def _small_dft(n):
    jk = (np.arange(n, dtype=np.int64)[:, None] * np.arange(n, dtype=np.int64)[None, :]) % n
    ang = 2.0 * np.pi * jk / n
    return np.cos(ang), np.sin(ang)


def _channel_dft(f, cb, sb):
    xc = jnp.dot(f, cb, preferred_element_type=F32).astype(BF16)
    xs = jnp.dot(f, sb, preferred_element_type=F32).astype(BF16)
    return xc, xs


def _fourier_ctx_kernel(scale, f_ref, cb_ref, sb_ref, ct_ref, st_ref, o_ref):
    xc, xs = _channel_dft(f_ref[...], cb_ref[...], sb_ref[...])
    y = (jnp.dot(ct_ref[...], xc, preferred_element_type=F32)
         - jnp.dot(st_ref[...], xs, preferred_element_type=F32))
    o_ref[...] = (y * scale).astype(o_ref.dtype)


def _fourier_lat_kernel(scale, f_ref, cb_ref, sb_ref, ct_ref, st_ref, o_ref, xc_ref, xs_ref):
    @pl.when(pl.program_id(1) == 0)
    def _():
        xc, xs = _channel_dft(f_ref[...], cb_ref[...], sb_ref[...])
        xc_ref[...] = xc
        xs_ref[...] = xs

    y = (jnp.dot(ct_ref[...], xc_ref[...], preferred_element_type=F32)
         - jnp.dot(st_ref[...], xs_ref[...], preferred_element_type=F32))
    o_ref[...] = (y * scale).astype(o_ref.dtype)


def _fourier(f, cb, sb, ct_ctx, st_ctx, ct_lat, st_lat, n_ctx, ctx_seq, dec_seq):
    n, width = f.shape
    full = lambda *_: (0, 0)
    ctx_out = pl.pallas_call(
        functools.partial(_fourier_ctx_kernel, 1.0 / math.sqrt(ctx_seq * FOURIER_GROUP_DIM)),
        grid=(n_ctx // ctx_seq,),
        in_specs=[pl.BlockSpec((ctx_seq, width), lambda b: (b, 0)),
                  pl.BlockSpec((width, width), full),
                  pl.BlockSpec((width, width), full),
                  pl.BlockSpec((ctx_seq, ctx_seq), full),
                  pl.BlockSpec((ctx_seq, ctx_seq), full)],
        out_specs=pl.BlockSpec((ctx_seq, width), lambda b: (b, 0)),
        out_shape=jax.ShapeDtypeStruct((n_ctx, width), BF16),
        compiler_params=_cparams("parallel"),
        name="fourier_ctx",
    )(f, cb, sb, ct_ctx, st_ctx)
    tr = TOKEN_BLOCK
    dec_batch = (n - n_ctx) // dec_seq
    fidx = lambda b, r: (n_ctx // dec_seq + b, 0)
    oidx = lambda b, r: (b * (dec_seq // tr) + r, 0)
    lat_out = pl.pallas_call(
        functools.partial(_fourier_lat_kernel, 1.0 / math.sqrt(dec_seq * FOURIER_GROUP_DIM)),
        grid=(dec_batch, dec_seq // tr),
        in_specs=[pl.BlockSpec((dec_seq, width), fidx),
                  pl.BlockSpec((width, width), full),
                  pl.BlockSpec((width, width), full),
                  pl.BlockSpec((tr, dec_seq), lambda b, r: (r, 0)),
                  pl.BlockSpec((tr, dec_seq), lambda b, r: (r, 0))],
        out_specs=pl.BlockSpec((tr, width), oidx),
        out_shape=jax.ShapeDtypeStruct((n - n_ctx, width), BF16),
        scratch_shapes=[pltpu.VMEM((dec_seq, width), BF16)] * 2,
        compiler_params=_cparams("parallel", "arbitrary"),
        name="fourier_lat",
    )(f, cb, sb, ct_lat, st_lat)
    return ctx_out, lat_out


K1 = PEER_TOPK + 1
KEY_VREGS = PEER_KEYS // SUBLANES
LIST_VREGS = -(-K1 // SUBLANES)


def _oddeven_merge_sort_pairs(n):
    pairs = []
    p = 1
    while p < n:
        k = p
        while k >= 1:
            for j in range(k % p, n - k, 2 * k):
                for i in range(min(k, n - j - k)):
                    if (i + j) // (2 * p) == (i + j + k) // (2 * p):
                        pairs.append((i + j, i + j + k))
            k //= 2
        p *= 2
    return pairs


_SORT_PAIRS = _oddeven_merge_sort_pairs(KEY_VREGS)


def _all_sublanes(x, op):
    for shift in (4, 2, 1):
        x = op(x, pltpu.roll(x, shift, 0))
    return x


def _row_iota():
    return lax.broadcasted_iota(jnp.int32, (SUBLANES, LANES), 0).astype(F32)


def _sorted_top(tiles):
    v = list(tiles)
    for i, j in _SORT_PAIRS:
        v[i], v[j] = jnp.maximum(v[i], v[j]), jnp.minimum(v[i], v[j])
    row8 = _row_iota()
    cnt = jnp.zeros((SUBLANES, LANES), F32)
    outs = [jnp.full((SUBLANES, LANES), NEG_INF, F32) for _ in range(LIST_VREGS)]
    for r in range(K1):
        depth = min(KEY_VREGS, K1 - r)
        m = _all_sublanes(v[0], jnp.maximum)
        eq = v[0] == m
        new = cnt + _all_sublanes(jnp.where(eq, 1.0, 0.0), jnp.add)
        for k in range(LIST_VREGS):
            if SUBLANES * k + SUBLANES - 1 >= r:
                rowk = row8 + float(SUBLANES * k)
                outs[k] = jnp.where(rowk >= cnt, jnp.where(rowk < new, m, outs[k]), outs[k])
        for d in range(depth - 1):
            v[d] = jnp.where(eq, v[d + 1], v[d])
        v[depth - 1] = jnp.where(eq, NEG_INF, v[depth - 1])
        cnt = new
    last = LIST_VREGS - 1
    outs[last] = jnp.where(row8 + float(SUBLANES * last) < float(K1), outs[last], NEG_INF)
    return outs


def _list_entry(lst, q):
    return jnp.broadcast_to(lst[q // SUBLANES][q % SUBLANES:q % SUBLANES + 1, :], (SUBLANES, LANES))


def _threshold_and_norm(a, b_vals):
    row8 = _row_iota()
    stack = [jnp.where(row8 < float(K1 // (q + 1)), a[0] + b_vals[q], NEG_INF) for q in range(K1)]
    rest = [a[k] + b_vals[0] for k in range(1, LIST_VREGS)]
    cmax = _list_entry(a, 0) + b_vals[0]
    cnt = jnp.zeros((SUBLANES, LANES), F32)
    hi = jnp.full((SUBLANES, LANES), NEG_INF, F32)
    lo = jnp.full((SUBLANES, LANES), NEG_INF, F32)
    z = jnp.zeros((SUBLANES, LANES), F32)
    for r in range(K1):
        depth = K1 - r
        top = stack[0]
        for t in rest:
            top = jnp.maximum(top, t)
        m = _all_sublanes(top, jnp.maximum)
        eq0 = stack[0] == m
        ones = jnp.where(eq0, 1.0, 0.0)
        for t in rest:
            ones = ones + jnp.where(t == m, 1.0, 0.0)
        c = _all_sublanes(ones, jnp.add)
        new = cnt + c
        z = z + jnp.where(cnt < float(PEER_TOPK), c * jnp.exp(m - cmax), 0.0)
        hi = jnp.where(cnt < float(PEER_TOPK), jnp.where(new >= float(PEER_TOPK), m, hi), hi)
        lo = jnp.where(cnt < float(K1), jnp.where(new >= float(K1), m, lo), lo)
        for d in range(depth - 1):
            stack[d] = jnp.where(eq0, stack[d + 1], stack[d])
        stack[depth - 1] = jnp.where(eq0, NEG_INF, stack[depth - 1])
        rest = [jnp.where(t == m, NEG_INF, t) for t in rest]
        cnt = new
    return 0.5 * (hi + lo), z


def _select_tile(s1, s2):
    a = _sorted_top(s1)
    b = _sorted_top(s2)
    b_vals = [_list_entry(b, q) for q in range(K1)]
    tmid, z = _threshold_and_norm(a, b_vals)
    a0 = _list_entry(a, 0)
    inv_z = 0.5 / z
    counts, ranks, e1, e2 = [], [], [], []
    for t1, t2 in zip(s1, s2):
        thr = tmid - t1
        cnt = jnp.zeros_like(t1)
        rank = jnp.zeros_like(t2)
        for q in range(PEER_TOPK):
            cnt = jnp.where(b_vals[q] >= thr, float(q + 1), cnt)
            rank = jnp.where(b_vals[q] > t2, float(q + 1), rank)
        counts.append(cnt)
        ranks.append(rank)
        e1.append(jnp.exp(t1 - a0) * inv_z)
        e2.append(jnp.exp(t2 - b_vals[0]))
    return counts, ranks, e1, e2


def _pack_pairs(tiles):
    words = [pltpu.bitcast(jnp.concatenate(tiles[c:c + 2], axis=0).astype(BF16), jnp.uint32)
             for c in range(0, len(tiles), 2)]
    return jnp.concatenate(words, axis=0)


def _peer_select_kernel(x_ref, mod_ref, g_ref, wq_ref, keys_ref, h_ref, cnt_ref, e1_ref, rank_ref, e2_ref, q_scr):
    h2 = _modulate(x_ref[...], g_ref[...], mod_ref[0, 3:4, :], mod_ref[0, 4:5, :]).astype(BF16)
    h_ref[...] = h2
    q = jnp.dot(h2, wq_ref[...], preferred_element_type=F32).astype(BF16)
    for c in range(2 * PEER_HEADS):
        q_scr[c] = q[:, c * PEER_HALF:(c + 1) * PEER_HALF]

    def head(h, carry):
        s1 = lax.dot_general(keys_ref[2 * h], q_scr[2 * h], _NT, preferred_element_type=F32)
        s2 = lax.dot_general(keys_ref[2 * h + 1], q_scr[2 * h + 1], _NT, preferred_element_type=F32)
        for lt in range(s1.shape[1] // LANES):
            lanes = slice(lt * LANES, (lt + 1) * LANES)
            tiles = lambda s: [s[SUBLANES * v:SUBLANES * (v + 1), lanes] for v in range(KEY_VREGS)]
            counts, ranks, e1, e2 = _select_tile(tiles(s1), tiles(s2))
            cnt_ref[h, :, lanes] = jnp.concatenate(counts, axis=0)
            e1_ref[h, :, lanes] = jnp.concatenate(e1, axis=0)
            rank_ref[h, :, lanes] = _pack_pairs(ranks)
            e2_ref[h, :, lanes] = _pack_pairs(e2)
        return carry

    lax.fori_loop(0, PEER_HEADS, head, 0)


def _peer_select(x, mod, g, wq_bf, keys_bf, n_ctx, dec_seq):
    n, d = x.shape
    tm = SELECT_TOKENS
    ncb, lb = n_ctx // tm, dec_seq // tm
    row = lambda i: (i, 0)
    sel_spec = pl.BlockSpec((PEER_HEADS, PEER_KEYS, tm), lambda i: (0, 0, i))
    sel_f32 = jax.ShapeDtypeStruct((PEER_HEADS, PEER_KEYS, n), F32)
    packed = jax.ShapeDtypeStruct((PEER_HEADS, PEER_KEYS // 2, n), jnp.uint32)
    packed_spec = pl.BlockSpec((PEER_HEADS, PEER_KEYS // 2, tm), lambda i: (0, 0, i))
    return pl.pallas_call(
        _peer_select_kernel,
        grid=(n // tm,),
        in_specs=[pl.BlockSpec((tm, d), row),
                  pl.BlockSpec((1, N_MOD, d), lambda i: (_group_of_block(i, ncb, lb), 0, 0)),
                  pl.BlockSpec((1, d), lambda i: (0, 0)),
                  pl.BlockSpec(wq_bf.shape, lambda i: (0, 0)),
                  pl.BlockSpec(keys_bf.shape, lambda i: (0, 0, 0))],
        out_specs=[pl.BlockSpec((tm, d), row), sel_spec, sel_spec, packed_spec, packed_spec],
        out_shape=[jax.ShapeDtypeStruct((n, d), BF16), sel_f32, sel_f32, packed, packed],
        scratch_shapes=[pltpu.VMEM((2 * PEER_HEADS, tm, PEER_HALF), BF16)],
        compiler_params=_cparams("parallel"),
        name="peer_select",
    )(x, mod, g, wq_bf, keys_bf)


def _twice_gelu_tanh(x):
    c = math.sqrt(2.0 / math.pi)
    return x * (1.0 + jnp.tanh(x * (c + (c * 0.044715) * (x * x))))


PACKED_ROWS = 2 * SUBLANES


def _peer_dense_kernel(x_ref, mod_ref, h_ref, u_ref, vt_ref, cnt_ref, e1_ref, rank_ref, e2_ref, o_ref,
                       acc_ref, act_ref, gt_ref):
    eb = pl.program_id(1)

    @pl.when(eb == 0)
    def _():
        acc_ref[...] = jnp.zeros_like(acc_ref)

    h2 = h_ref[...]
    tb = h2.shape[0]
    zero = jnp.zeros((), BF16)
    act_ref[...] = lax.dot_general(u_ref[...], h2, _NT, preferred_element_type=F32)

    for il in range(DENSE_EXPERTS // PEER_KEYS):
        for lt in range(tb // LANES):
            lanes = slice(lt * LANES, (lt + 1) * LANES)
            bcast = lambda ref, h: jnp.broadcast_to(ref[h, il:il + 1, lanes], (PACKED_ROWS, LANES)).astype(BF16)
            n_chunks = PEER_KEYS // PACKED_ROWS
            w = [None] * n_chunks
            for h in range(PEER_HEADS):
                cnt, e1 = bcast(cnt_ref, h), bcast(e1_ref, h)
                for rc in range(n_chunks):
                    jr = slice(rc * SUBLANES, (rc + 1) * SUBLANES)
                    rank = pltpu.bitcast(rank_ref[h, jr, lanes], BF16)
                    term = jnp.where(rank < cnt, pltpu.bitcast(e2_ref[h, jr, lanes], BF16), zero) * e1
                    w[rc] = term if w[rc] is None else w[rc] + term
            for rc in range(n_chunks):
                er = slice(il * PEER_KEYS + rc * PACKED_ROWS, il * PEER_KEYS + (rc + 1) * PACKED_ROWS)
                gt_ref[er, lanes] = w[rc] * _twice_gelu_tanh(act_ref[er, lanes].astype(BF16))

    acc_ref[...] += jnp.dot(vt_ref[0], gt_ref[...], preferred_element_type=F32)

    @pl.when(eb == pl.num_programs(1) - 1)
    def _():
        o_ref[...] = x_ref[...] + mod_ref[0, 5:6, :] * acc_ref[...].T


def _peer_dense(x, mod, h2, u_bf, vt_bf, cnt, e1, rank, e2, n_ctx, dec_seq):
    n, d = x.shape
    tb, eb = DENSE_TOKENS, DENSE_EXPERTS
    n_exp = u_bf.shape[0]
    ncb, lb = n_ctx // tb, dec_seq // tb
    row = lambda t, e: (t, 0)
    key_spec = pl.BlockSpec((PEER_HEADS, eb // PEER_KEYS, tb), lambda t, e: (0, e, t))
    packed_spec = pl.BlockSpec((PEER_HEADS, PEER_KEYS // 2, tb), lambda t, e: (0, 0, t))
    return pl.pallas_call(
        _peer_dense_kernel,
        grid=(n // tb, n_exp // eb),
        in_specs=[pl.BlockSpec((tb, d), row),
                  pl.BlockSpec((1, N_MOD, d), lambda t, e: (_group_of_block(t, ncb, lb), 0, 0)),
                  pl.BlockSpec((tb, d), row),
                  pl.BlockSpec((eb, d), lambda t, e: (e, 0)),
                  pl.BlockSpec((1, d, eb), lambda t, e: (e, 0, 0)),
                  key_spec, key_spec, packed_spec, packed_spec],
        out_specs=pl.BlockSpec((tb, d), row),
        out_shape=jax.ShapeDtypeStruct((n, d), F32),
        scratch_shapes=[pltpu.VMEM((d, tb), F32),
                        pltpu.VMEM((eb, tb), F32),
                        pltpu.VMEM((eb, tb), BF16)],
        compiler_params=_cparams("parallel", "arbitrary"),
        name="peer_dense",
    )(x, mod, h2, u_bf, vt_bf, cnt, e1, rank, e2)


def _final_norm_kernel(x_ref, g_ref, o_ref):
    x = x_ref[...]
    ms = jnp.mean(x * x, axis=-1, keepdims=True)
    o_ref[...] = x * lax.rsqrt(ms + EPS) * g_ref[...]


def _final_norm(x, g, first_row, n_rows):
    d = x.shape[1]
    tm = TOKEN_BLOCK
    first_block = first_row // tm
    return pl.pallas_call(
        _final_norm_kernel,
        grid=(n_rows // tm,),
        in_specs=[pl.BlockSpec((tm, d), lambda i: (first_block + i, 0)), pl.BlockSpec((1, d), lambda i: (0, 0))],
        out_specs=pl.BlockSpec((tm, d), lambda i: (i, 0)),
        out_shape=jax.ShapeDtypeStruct((n_rows, d), F32),
        compiler_params=_cparams("parallel"),
        name="final_norm",
    )(x, g)


def _rope_tables(n_tokens):
    rows = n_tokens // GRID_W
    row = np.repeat(np.arange(rows), GRID_W).astype(np.float32)
    col = np.tile(np.arange(GRID_W), rows).astype(np.float32)
    half = HEAD_DIM // 2
    inv = (np.float32(ROPE_THETA) ** (-np.arange(0, half, 2, dtype=np.float32) / np.float32(half))).astype(np.float32)
    ang = np.concatenate([row[:, None] * inv, col[:, None] * inv], axis=-1).astype(np.float32)
    cos = np.repeat(np.cos(ang.astype(np.float64)), 2, axis=-1)
    sin = np.repeat(np.sin(ang.astype(np.float64)), 2, axis=-1)
    sin = sin * np.tile(np.array([-1.0, 1.0]), HEAD_DIM // 2)
    return (jnp.asarray(np.tile(cos, (1, N_HEADS)), F32), jnp.asarray(np.tile(sin, (1, N_HEADS)), F32))


def _block_diag(block, count):
    n = block.shape[0]
    out = np.zeros((n * count, n * count), np.float64)
    for c in range(count):
        out[c * n:(c + 1) * n, c * n:(c + 1) * n] = block
    return out


def kernel(x_prompt, x_sample, cache_k, cache_v, c, c_ctx, w_mod, b_mod, norm_mix_g, norm_ffn_g, w_in_even, w_out_even, conv_dw_w, conv_dw_b, conv_ln_g, conv_ln_b, q_norm_g, k_norm_g, w_in_odd, w_out_odd, short_conv_w, peer_wq, peer_keys, peer_u, peer_v, final_norm_g):
    batch, ctx_seq, d = x_prompt.shape
    dec_batch, dec_seq, _ = x_sample.shape
    depth = w_mod.shape[0]
    n_ctx = batch * ctx_seq
    past = cache_k.shape[2]

    x = jnp.concatenate([x_prompt.reshape(n_ctx, d), x_sample.reshape(dec_batch * dec_seq, d)], axis=0)
    mods = _modulation(jnp.concatenate([c_ctx[None, :], c], axis=0), w_mod, b_mod)

    cos, sin = _rope_tables(dec_seq)
    bd = jnp.asarray(_block_diag(np.ones((HEAD_DIM, HEAD_DIM)), N_HEADS), BF16)
    c128, s128 = _small_dft(FOURIER_GROUP_DIM)
    cb = jnp.asarray(_block_diag(c128, FOURIER_GROUPS), BF16)
    sb = jnp.asarray(_block_diag(s128, FOURIER_GROUPS), BF16)
    c_ctx_t, s_ctx_t = _small_dft(ctx_seq)
    ct_ctx, st_ctx = jnp.asarray(c_ctx_t, BF16), jnp.asarray(s_ctx_t, BF16)
    if depth > 1:
        ct_lat, st_lat = _dft_tables(dec_seq)
    ck = cache_k.reshape(dec_batch, -1, past, KV_WIDTH)
    cv = cache_v.reshape(dec_batch, -1, past, KV_WIDTH)

    new_k, new_v = [], []
    for l in range(depth):
        mod = mods[l]
        j = l // 2
        g_mix = norm_mix_g[l][None, :]
        if l % 2 == 0:
            qg = jnp.tile(q_norm_g[j], N_HEADS)[None, :]
            kg = jnp.tile(k_norm_g[j], N_KV_HEADS)[None, :]
            a, q, k, v = _even_in(x, mod, g_mix, w_in_even[j].astype(BF16), qg, kg, bd, cos, sin, n_ctx, dec_seq)
            new_k.append(k[:n_ctx].reshape(batch, ctx_seq, N_KV_HEADS, HEAD_DIM))
            new_v.append(v[:n_ctx].reshape(batch, ctx_seq, N_KV_HEADS, HEAD_DIM))
            y1 = _conv_module(a, conv_dw_w[j], conv_dw_b[j][None, :], conv_ln_g[j][None, :],
                              conv_ln_b[j][None, :], n_ctx, ctx_seq, dec_seq)
            y_ctx, y_lat = _attention(q, k, v, ck, cv, j, n_ctx, ctx_seq, dec_seq)
            x = _out_proj(x, mod, y1, y_ctx, y_lat, w_out_even[j].astype(BF16), 0, n_ctx, dec_seq)
        else:
            f, gb, cvv = _odd_in(x, mod, g_mix, w_in_odd[j].astype(BF16), n_ctx, dec_seq)
            y_ctx, y_lat = _fourier(f, cb, sb, ct_ctx, st_ctx, ct_lat, st_lat, n_ctx, ctx_seq, dec_seq)
            y2 = _short_conv(cvv, gb, short_conv_w[j], n_ctx, ctx_seq, dec_seq)
            x = _out_proj(x, mod, y2, y_ctx, y_lat, w_out_odd[j].astype(BF16), 1, n_ctx, dec_seq)
        keys_bf = peer_keys[l].reshape(2 * PEER_HEADS, PEER_KEYS, PEER_HALF).astype(BF16)
        h2, cnt, e1, rank, e2 = _peer_select(x, mod, norm_ffn_g[l][None, :], peer_wq[l].astype(BF16), keys_bf,
                                             n_ctx, dec_seq)
        vt_blocks = peer_v[l].reshape(-1, DENSE_EXPERTS, d).transpose(0, 2, 1).astype(BF16)
        x = _peer_dense(x, mod, h2, peer_u[l].astype(BF16), vt_blocks, cnt, e1, rank, e2,
                        n_ctx, dec_seq)

    y_prompt = _final_norm(x, final_norm_g[None, :], 0, n_ctx).reshape(batch, ctx_seq, d)
    y_sample = _final_norm(x, final_norm_g[None, :], n_ctx, dec_batch * dec_seq).reshape(dec_batch, dec_seq, d)
    return (y_prompt, y_sample, jnp.stack(new_k, axis=1), jnp.stack(new_v, axis=1))
```
